```python
import math
import jax, jax.numpy as jnp
from jax import lax
import numpy as np

D_MODEL = 1024
BATCH = 4
SEQ = 4096
DEPTH = 2
DEC_BATCH = 32
DEC_SEQ = 8
PAST_LEN = 8192
PAGE_SIZE = 128

MIX_WIDTH = D_MODEL
CONV_CH = MIX_WIDTH // 4
CONV_GROUPS = 4
CONV_K = 3
HEAD_DIM = 64
ATTN_WIDTH = MIX_WIDTH - CONV_CH
N_HEADS = ATTN_WIDTH // HEAD_DIM
DILATED_PATTERNS = ((128, 1), (512, 4), (2048, 16))
MAX_WINDOW = 2048
N_BUCKETS = 32
MAX_DISTANCE = 2048
QBLK = 128
PLE_DIM = 256
N_KEYS = 128
N_EXPERTS = N_KEYS * N_KEYS
PEER_HEADS = 8
PEER_TOPK = 16
PEER_KEY_DIM = 256
PEER_HALF = PEER_KEY_DIM // 2
PEER_BLOCK = 128
EPS = 1e-6
NEG = -1e30

kernel_name = "hymba_conv_dilated_peer_decoder_step"


def rmsnorm(x, g):
    xf = x.astype(jnp.float32)
    y = xf * lax.rsqrt(jnp.mean(xf * xf, axis=-1, keepdims=True) + EPS) * g.astype(jnp.float32)
    return y.astype(x.dtype)


def t5_bucket(dist):
    max_exact = N_BUCKETS // 2
    df = jnp.maximum(dist, 1).astype(jnp.float32)
    large = max_exact + (jnp.log(df / max_exact) / math.log(MAX_DISTANCE / max_exact)
                         * (N_BUCKETS - max_exact)).astype(jnp.int32)
    large = jnp.minimum(large, N_BUCKETS - 1)
    return jnp.where(dist < max_exact, dist, large)


def dilated_attention(q, kbuf, vbuf, qidx, rel_bias):
    scale = HEAD_DIM ** -0.5
    ms, ls, os_ = [], [], []
    for window, dil in DILATED_PATTERNS:
        n = window // dil + 1
        dist = jnp.arange(n, dtype=jnp.int32) * dil
        idx = qidx[:, None] - dist[None, :]
        valid = idx >= 0
        idxc = jnp.maximum(idx, 0)
        kg = jnp.take(kbuf, idxc, axis=1)
        vg = jnp.take(vbuf, idxc, axis=1)
        bias = rel_bias[t5_bucket(dist)].astype(jnp.float32).T
        s = jnp.einsum('bthd,btnhd->bhtn', q, kg).astype(jnp.float32) * scale
        s = s + bias[None, :, None, :]
        s = jnp.where(valid[None, None], s, NEG)
        m = jnp.max(s, axis=-1)
        p = jnp.exp(s - m[..., None])
        ls.append(jnp.sum(p, axis=-1))
        ms.append(m)
        os_.append(jnp.einsum('bhtn,btnhd->bhtd', p, vg.astype(jnp.float32)))
    m_all = jnp.stack(ms)
    l_all = jnp.stack(ls)
    o_all = jnp.stack(os_)
    mx = jnp.max(m_all, axis=0)
    w = jnp.exp(m_all - mx[None])
    num = jnp.sum(w[..., None] * o_all, axis=0)
    den = jnp.sum(w * l_all, axis=0)
    out = num / den[..., None]
    return out.transpose(0, 2, 1, 3).astype(q.dtype)


def token_mixer(xn, w_in, conv_w, w_out, rel_bias, conv_past, k_past, v_past):
    B, T, _ = xn.shape
    proj = xn @ w_in
    c, a = CONV_CH, ATTN_WIDTH
    hc, bg, cg, q, k, v = jnp.split(proj, [c, 2 * c, 3 * c, 3 * c + a, 3 * c + 2 * a], axis=-1)
    u = cg * hc
    ucat = jnp.concatenate([conv_past.astype(u.dtype), u], axis=1)
    conv = sum(conv_w[j] * ucat[:, j:j + T] for j in range(CONV_K))
    y_conv = bg * conv
    q = q.reshape(B, T, N_HEADS, HEAD_DIM)
    k = k.reshape(B, T, N_HEADS, HEAD_DIM)
    v = v.reshape(B, T, N_HEADS, HEAD_DIM)
    L = k_past.shape[1]
    kbuf = jnp.concatenate([k_past.astype(k.dtype), k], axis=1)
    vbuf = jnp.concatenate([v_past.astype(v.dtype), v], axis=1)
    if T > QBLK and T % QBLK == 0:
        nb = T // QBLK
        qb = q.reshape(B, nb, QBLK, N_HEADS, HEAD_DIM).transpose(1, 0, 2, 3, 4)

        def body(args):
            qblk, bid = args
            qidx = L + bid * QBLK + jnp.arange(QBLK, dtype=jnp.int32)
            return dilated_attention(qblk, kbuf, vbuf, qidx, rel_bias)

        ya = lax.map(body, (qb, jnp.arange(nb, dtype=jnp.int32)))
        ya = ya.transpose(1, 0, 2, 3, 4).reshape(B, T, N_HEADS, HEAD_DIM)
    else:
        ya = dilated_attention(q, kbuf, vbuf, L + jnp.arange(T, dtype=jnp.int32), rel_bias)
    out = jnp.concatenate([y_conv, ya.reshape(B, T, ATTN_WIDTH)], axis=-1) @ w_out
    return out, ucat[:, -(CONV_K - 1):], k, v


def peer_block(xt, w_q, k1, k2, u_tab, v_tab):
    n = xt.shape[0]
    q = (xt @ w_q).reshape(n, PEER_HEADS, 2, PEER_HALF)
    s1 = jnp.einsum('nhc,kc->nhk', q[:, :, 0], k1).astype(jnp.float32)
    s2 = jnp.einsum('nhc,kc->nhk', q[:, :, 1], k2).astype(jnp.float32)
    t1, i1 = lax.top_k(s1, PEER_TOPK)
    t2, i2 = lax.top_k(s2, PEER_TOPK)
    cand = (t1[..., :, None] + t2[..., None, :]).reshape(n, PEER_HEADS, PEER_TOPK * PEER_TOPK)
    cidx = (i1[..., :, None] * N_KEYS + i2[..., None, :]).reshape(n, PEER_HEADS, PEER_TOPK * PEER_TOPK)
    top, pos = lax.top_k(cand, PEER_TOPK)
    eidx = jnp.take_along_axis(cidx, pos, axis=-1)
    g = jax.nn.softmax(top, axis=-1)
    u = u_tab[eidx]
    act = jax.nn.gelu(jnp.einsum('nd,nhkd->nhk', xt, u).astype(jnp.float32), approximate=False)
    v = v_tab[eidx]
    return jnp.einsum('nhk,nhkd->nd', (g * act).astype(xt.dtype), v)


def peer(xn, w_q, k1, k2, u_tab, v_tab):
    B, T, D = xn.shape
    toks = xn.reshape(B * T, D)
    ntok = B * T
    if ntok > PEER_BLOCK and ntok % PEER_BLOCK == 0:
        blocks = toks.reshape(ntok // PEER_BLOCK, PEER_BLOCK, D)
        out = lax.map(lambda xb: peer_block(xb, w_q, k1, k2, u_tab, v_tab), blocks)
    else:
        out = peer_block(toks, w_q, k1, k2, u_tab, v_tab)
    return out.reshape(B, T, D)


def layer(h, p, i, conv_past, k_past, v_past, g_mix, w_in, conv_w, w_out, rel_bias, g_ffn,
          peer_wq, peer_k1, peer_k2, peer_u, peer_v, g_ple, w_ple_gate, w_ple_proj):
    xn = rmsnorm(h, g_mix[i])
    mix, conv_state, k_new, v_new = token_mixer(xn, w_in[i], conv_w[i], w_out[i], rel_bias,
                                                conv_past, k_past, v_past)
    h = h + mix
    h = h + peer(rmsnorm(h, g_ffn[i]), peer_wq[i], peer_k1[i], peer_k2[i], peer_u[i], peer_v[i])
    gate = jax.nn.sigmoid(rmsnorm(h, g_ple[i]) @ w_ple_gate[i])
    h = h + (p @ w_ple_proj[i]) * gate
    return h, conv_state, k_new, v_new


def setup_inputs(seed: int = 0) -> dict:
    key = jax.random.key(seed)
    ks = jax.random.split(key, 24)
    f32 = jnp.float32
    nrm = lambda k, shape, s: jax.random.normal(k, shape, f32) * s
    wbuf = min(MAX_WINDOW, PAST_LEN)
    return {
        "x_prompt": nrm(ks[0], (BATCH, SEQ, D_MODEL), 1.0),
        "x_sample": nrm(ks[1], (DEC_BATCH, DEC_SEQ, D_MODEL), 1.0),
        "cache_k": nrm(ks[2], (DEPTH, DEC_BATCH, wbuf, N_HEADS, HEAD_DIM), 1.0),
        "cache_v": nrm(ks[3], (DEPTH, DEC_BATCH, wbuf, N_HEADS, HEAD_DIM), 1.0),
        "state_conv": nrm(ks[4], (DEPTH, DEC_BATCH, CONV_K - 1, CONV_CH), 1.0),
        "p_prompt": nrm(ks[5], (DEPTH, BATCH, SEQ, PLE_DIM), 1.0),
        "p_sample": nrm(ks[6], (DEPTH, DEC_BATCH, DEC_SEQ, PLE_DIM), 1.0),
        "g_mix": 1.0 + nrm(ks[7], (DEPTH, D_MODEL), 0.05),
        "w_in": nrm(ks[8], (DEPTH, D_MODEL, 3 * MIX_WIDTH), D_MODEL ** -0.5),
        "conv_w": nrm(ks[9], (DEPTH, CONV_K, CONV_CH), CONV_K ** -0.5),
        "w_out": nrm(ks[10], (DEPTH, MIX_WIDTH, D_MODEL), MIX_WIDTH ** -0.5),
        "rel_bias": nrm(ks[11], (N_BUCKETS, N_HEADS), 0.5),
        "g_ffn": 1.0 + nrm(ks[12], (DEPTH, D_MODEL), 0.05),
        "peer_wq": nrm(ks[13], (DEPTH, D_MODEL, PEER_HEADS * PEER_KEY_DIM), D_MODEL ** -0.5),
        "peer_k1": nrm(ks[14], (DEPTH, N_KEYS, PEER_HALF), PEER_HALF ** -0.5),
        "peer_k2": nrm(ks[15], (DEPTH, N_KEYS, PEER_HALF), PEER_HALF ** -0.5),
        "peer_u": nrm(ks[16], (DEPTH, N_EXPERTS, D_MODEL), D_MODEL ** -0.5),
        "peer_v": nrm(ks[17], (DEPTH, N_EXPERTS, D_MODEL), PEER_HEADS ** -0.5),
        "g_ple": 1.0 + nrm(ks[18], (DEPTH, D_MODEL), 0.05),
        "w_ple_gate": nrm(ks[19], (DEPTH, D_MODEL, D_MODEL), D_MODEL ** -0.5),
        "w_ple_proj": nrm(ks[20], (DEPTH, PLE_DIM, D_MODEL), PLE_DIM ** -0.5),
        "g_final": 1.0 + nrm(ks[21], (D_MODEL,), 0.05),
    }


def reference(x_prompt, x_sample, cache_k, cache_v, state_conv, p_prompt, p_sample,
              g_mix, w_in, conv_w, w_out, rel_bias, g_ffn, peer_wq, peer_k1, peer_k2,
              peer_u, peer_v, g_ple, w_ple_gate, w_ple_proj, g_final):
    B, T = x_prompt.shape[0], x_prompt.shape[1]
    keep = min(MAX_WINDOW, T)
    weights = (g_mix, w_in, conv_w, w_out, rel_bias, g_ffn, peer_wq, peer_k1, peer_k2,
               peer_u, peer_v, g_ple, w_ple_gate, w_ple_proj)
    hp, hs = x_prompt, x_sample
    kp_l, vp_l, cp_l, ks_l, vs_l, cs_l = [], [], [], [], [], []
    for i in range(DEPTH):
        conv0 = jnp.zeros((B, CONV_K - 1, CONV_CH), x_prompt.dtype)
        kv0 = jnp.zeros((B, 0, N_HEADS, HEAD_DIM), x_prompt.dtype)
        hp, cp, kp, vp = layer(hp, p_prompt[i], i, conv0, kv0, kv0, *weights)
        kp_l.append(kp[:, -keep:])
        vp_l.append(vp[:, -keep:])
        cp_l.append(cp)
        hs, cs, kn, vn = layer(hs, p_sample[i], i, state_conv[i], cache_k[i], cache_v[i], *weights)
        ks_l.append(kn)
        vs_l.append(vn)
        cs_l.append(cs)
    y_prompt = rmsnorm(hp, g_final)
    y_sample = rmsnorm(hs, g_final)
    return (y_prompt, y_sample, jnp.stack(kp_l), jnp.stack(vp_l), jnp.stack(cp_l),
            jnp.stack(ks_l), jnp.stack(vs_l), jnp.stack(cs_l))
```

```python
import functools
import math

import numpy as np
import jax
import jax.numpy as jnp
from jax import lax
from jax.experimental import pallas as pl
from jax.experimental.pallas import tpu as pltpu

F32 = jnp.float32
BF16 = jnp.bfloat16

HEAD_DIM = 64
LANES = 128
DILATED_PATTERNS = ((128, 1), (512, 4), (2048, 16))
BAND = 128
MAX_WINDOW = 2048
N_BUCKETS = 32
MAX_DISTANCE = 2048
N_KEYS = 128
PEER_HEADS = 8
PEER_TOPK = 16
EPS = 1e-6
NEG = -1e30
VMEM_LIMIT_BYTES = 56 * 1024 * 1024


def _cparams(*sem):
    return pltpu.CompilerParams(dimension_semantics=sem, vmem_limit_bytes=VMEM_LIMIT_BYTES)


def _row_tile(rows, cap):
    t = min(rows, cap)
    while rows % t:
        t //= 2
    return t


def _rms(x, g):
    return x * lax.rsqrt(jnp.mean(x * x, axis=-1, keepdims=True) + EPS) * g


def _inproj_kernel(x_ref, g_ref, w_ref, u_ref, bg_ref, q_ref, k_ref, v_ref, *, conv_ch, attn_w):
    xn = _rms(x_ref[...], g_ref[...])
    proj = jnp.dot(xn.astype(BF16), w_ref[...], preferred_element_type=F32)
    c, a = conv_ch, attn_w
    u_ref[...] = proj[:, 2 * c:3 * c] * proj[:, :c]
    bg_ref[...] = proj[:, c:2 * c]
    q_ref[...] = proj[:, 3 * c:3 * c + a]
    k_ref[...] = proj[:, 3 * c + a:3 * c + 2 * a]
    v_ref[...] = proj[:, 3 * c + 2 * a:]


def _inproj(x, g_all, w_all, layer, conv_ch, attn_w):
    rows, d = x.shape
    tm = _row_tile(rows, 512)
    n_out = w_all.shape[-1]
    row = lambda w: pl.BlockSpec((tm, w), lambda i: (i, 0))
    return pl.pallas_call(
        functools.partial(_inproj_kernel, conv_ch=conv_ch, attn_w=attn_w),
        grid=(rows // tm,),
        in_specs=[row(d),
                  pl.BlockSpec((None, 1, d), lambda i: (layer, 0, 0)),
                  pl.BlockSpec((None, d, n_out), lambda i: (layer, 0, 0))],
        out_specs=[row(conv_ch), row(conv_ch), row(attn_w), row(attn_w), row(attn_w)],
        out_shape=[jax.ShapeDtypeStruct((rows, conv_ch), F32)] * 2
        + [jax.ShapeDtypeStruct((rows, attn_w), F32)] * 3,
        compiler_params=_cparams("parallel"),
        name="inproj",
    )(x, g_all, w_all)


def _conv_kernel(u_ref, bg_ref, past_ref, w_ref, y_ref, st_ref, *, T):
    u = u_ref[...]
    row = lax.broadcasted_iota(jnp.int32, u.shape, 0)
    p0 = past_ref[0:1, :]
    p1 = past_ref[1:2, :]
    um1 = jnp.where(row == 0, p1, pltpu.roll(u, 1, 0))
    um2 = jnp.where(row == 0, p0, jnp.where(row == 1, p1, pltpu.roll(u, 2, 0)))
    conv = w_ref[0:1, :] * um2 + w_ref[1:2, :] * um1 + w_ref[2:3, :] * u
    y_ref[...] = bg_ref[...] * conv
    st_ref[...] = u_ref[T - 2:T, :]


def _conv(u, bg, past, w_all, layer, nseq, T):
    rows, c = u.shape
    blk = pl.BlockSpec((T, c), lambda s: (s, 0))
    return pl.pallas_call(
        functools.partial(_conv_kernel, T=T),
        grid=(nseq,),
        in_specs=[blk, blk,
                  pl.BlockSpec((None, 2, c), lambda s: (s, 0, 0)),
                  pl.BlockSpec((None, 3, c), lambda s: (layer, 0, 0))],
        out_specs=[blk, pl.BlockSpec((None, 2, c), lambda s: (s, 0, 0))],
        out_shape=[jax.ShapeDtypeStruct((rows, c), F32),
                   jax.ShapeDtypeStruct((nseq, 2, c), F32)],
        compiler_params=_cparams("parallel"),
        name="conv",
    )(u, bg, past, w_all)


def _t5_bucket_np(dist):
    max_exact = N_BUCKETS // 2
    df = np.maximum(dist, 1).astype(np.float32)
    large = max_exact + (np.log(df / np.float32(max_exact)) / np.float32(math.log(MAX_DISTANCE / max_exact))
                         * np.float32(N_BUCKETS - max_exact)).astype(np.int32)
    large = np.minimum(large, N_BUCKETS - 1)
    return np.where(dist < max_exact, dist, large).astype(np.int32)


def _prompt_bias_tables(rel_bias, qb):
    kb = qb + BAND
    qi = np.arange(qb)[:, None]
    ki = np.arange(kb)[None, :]
    j = qi + BAND - ki
    idx = np.full((len(DILATED_PATTERNS), 2, qb, kb), N_BUCKETS, np.int32)
    for r, (_, dil) in enumerate(DILATED_PATTERNS):
        inband = (j >= 0) & (j <= BAND)
        b = _t5_bucket_np(np.clip(j, 0, BAND) * dil)
        idx[r, 0] = np.where(inband, b, N_BUCKETS)
        idx[r, 1] = np.where(inband & (ki >= BAND), b, N_BUCKETS)
    ext = jnp.concatenate([rel_bias.astype(F32), jnp.full((1, rel_bias.shape[1]), NEG, F32)], axis=0)
    tab = ext[jnp.asarray(idx)]
    return jnp.transpose(tab, (0, 1, 4, 2, 3))


def _sample_tables(rel_bias, L, T, lp):
    t = np.arange(T)[:, None]
    pos = np.arange(lp)[None, :]
    dist = L + t - pos
    valid = (dist >= 0) & (pos < L + T)
    mult = np.zeros((T, lp), np.float32)
    for window, dil in DILATED_PATTERNS:
        mult += (valid & (dist % dil == 0) & (dist <= window)).astype(np.float32)
    idx = _t5_bucket_np(np.clip(dist, 0, MAX_DISTANCE))
    bias = jnp.transpose(rel_bias.astype(F32)[jnp.asarray(idx)], (2, 0, 1))
    return bias, jnp.asarray(mult)


def _attn_prompt_kernel(q_ref, k_ref, v_ref, bias_ref, o_ref, kp, vp, m_sc, l_sc, acc_sc, *, T, QB, PAD):
    KB = QB + BAND
    zeros = jnp.zeros((PAD, LANES), F32)
    kp[0:PAD, :] = zeros
    vp[0:PAD, :] = zeros
    kp[PAD:PAD + T, :] = k_ref[...]
    vp[PAD:PAD + T, :] = v_ref[...]
    m_sc[...] = jnp.full((T, LANES), NEG, F32)
    l_sc[...] = jnp.zeros((T, LANES), F32)
    acc_sc[...] = jnp.zeros((T, LANES), F32)
    is0 = lax.broadcasted_iota(jnp.int32, (1, LANES), 1) < HEAD_DIM
    scale = HEAD_DIM ** -0.5

    for r, (_, d) in enumerate(DILATED_PATTERNS):
        nblk = (T // d) // QB

        def body(idx, carry, r=r, d=d, nblk=nblk):
            c = idx // nblk
            bi = idx % nblk
            a0 = bi * QB
            qstart = c + d * a0
            kstart = PAD + c + d * (a0 - BAND)
            if d == 1:
                qsl = pl.ds(pl.multiple_of(qstart, QB), QB)
                ksl = pl.ds(pl.multiple_of(kstart, BAND), KB)
            else:
                qsl = pl.ds(qstart, QB, stride=d)
                ksl = pl.ds(kstart, KB, stride=d)
            q = q_ref[qsl, :] * scale
            kb = kp[ksl, :].astype(BF16)
            vb = vp[ksl, :].astype(BF16)
            variant = jnp.where(bi == 0, 1, 0)
            m_old = m_sc[qsl, :]
            s, mx = [], []
            for hh in range(2):
                qh = jnp.where(is0 if hh == 0 else jnp.logical_not(is0), q, 0.0).astype(BF16)
                sh = lax.dot_general(qh, kb, (((1,), (1,)), ((), ())), preferred_element_type=F32)
                sh = sh + bias_ref[r, variant, hh]
                s.append(sh)
                mx.append(jnp.max(sh, axis=1, keepdims=True))
            m_new = jnp.maximum(m_old, jnp.where(is0, mx[0], mx[1]))
            alpha = jnp.exp(m_old - m_new)
            p0 = jnp.exp(s[0] - m_new[:, 0:1])
            p1 = jnp.exp(s[1] - m_new[:, HEAD_DIM:HEAD_DIM + 1])
            rs = jnp.where(is0, jnp.sum(p0, axis=1, keepdims=True), jnp.sum(p1, axis=1, keepdims=True))
            pv0 = jnp.dot(p0.astype(BF16), vb, preferred_element_type=F32)
            pv1 = jnp.dot(p1.astype(BF16), vb, preferred_element_type=F32)
            m_sc[qsl, :] = m_new
            l_sc[qsl, :] = alpha * l_sc[qsl, :] + rs
            acc_sc[qsl, :] = alpha * acc_sc[qsl, :] + jnp.where(is0, pv0, pv1)
            return carry

        lax.fori_loop(0, d * nblk, body, 0)

    o_ref[...] = (acc_sc[...] / l_sc[...]).astype(o_ref.dtype)


def _attn_prompt(q, k, v, bias_tab, B, T, QB):
    rows, aw = q.shape
    npair = aw // LANES
    PAD = BAND * max(d for _, d in DILATED_PATTERNS)
    KB = QB + BAND
    blk = pl.BlockSpec((T, LANES), lambda b, hp: (b, hp))
    return pl.pallas_call(
        functools.partial(_attn_prompt_kernel, T=T, QB=QB, PAD=PAD),
        grid=(B, npair),
        in_specs=[blk, blk, blk,
                  pl.BlockSpec((len(DILATED_PATTERNS), 2, 2, QB, KB), lambda b, hp: (0, 0, hp, 0, 0))],
        out_specs=blk,
        out_shape=jax.ShapeDtypeStruct((rows, aw), BF16),
        scratch_shapes=[pltpu.VMEM((PAD + T, LANES), F32), pltpu.VMEM((PAD + T, LANES), F32),
                        pltpu.VMEM((T, LANES), F32), pltpu.VMEM((T, LANES), F32), pltpu.VMEM((T, LANES), F32)],
        compiler_params=_cparams("parallel", "parallel"),
        name="attn_prompt",
    )(q, k, v, bias_tab)


def _attn_sample_kernel(q_ref, k_ref, v_ref, ck_ref, cv_ref, bias_ref, mult_ref, o_ref, kf, vf, *, L, T, LP):
    kf[0:L, :] = ck_ref[...]
    vf[0:L, :] = cv_ref[...]
    kf[L:LP, :] = jnp.zeros((LP - L, LANES), F32)
    vf[L:LP, :] = jnp.zeros((LP - L, LANES), F32)
    kf[L:L + T, :] = k_ref[...]
    vf[L:L + T, :] = v_ref[...]
    is0 = lax.broadcasted_iota(jnp.int32, (1, LANES), 1) < HEAD_DIM
    q = q_ref[...] * (HEAD_DIM ** -0.5)
    kb = kf[...].astype(BF16)
    vb = vf[...].astype(BF16)
    mult = mult_ref[...]
    outs = []
    for hh in range(2):
        qh = jnp.where(is0 if hh == 0 else jnp.logical_not(is0), q, 0.0).astype(BF16)
        s = lax.dot_general(qh, kb, (((1,), (1,)), ((), ())), preferred_element_type=F32)
        s = jnp.where(mult > 0.0, s + bias_ref[hh], NEG)
        m = jnp.max(s, axis=1, keepdims=True)
        p = jnp.exp(s - m) * mult
        l = jnp.sum(p, axis=1, keepdims=True)
        outs.append(jnp.dot(p.astype(BF16), vb, preferred_element_type=F32) / l)
    o_ref[...] = jnp.where(is0, outs[0], outs[1]).astype(o_ref.dtype)


def _attn_sample(q, k, v, cache_k, cache_v, bias, mult, layer, S, T):
    rows, aw = q.shape
    npair = aw // LANES
    L = cache_k.shape[2]
    LP = bias.shape[-1]
    new = pl.BlockSpec((T, LANES), lambda s, hp: (s, hp))
    cache = pl.BlockSpec((None, None, L, LANES), lambda s, hp: (layer, s, 0, hp))
    return pl.pallas_call(
        functools.partial(_attn_sample_kernel, L=L, T=T, LP=LP),
        grid=(S, npair),
        in_specs=[new, new, new, cache, cache,
                  pl.BlockSpec((2, T, LP), lambda s, hp: (hp, 0, 0)),
                  pl.BlockSpec((T, LP), lambda s, hp: (0, 0))],
        out_specs=new,
        out_shape=jax.ShapeDtypeStruct((rows, aw), BF16),
        scratch_shapes=[pltpu.VMEM((LP, LANES), F32), pltpu.VMEM((LP, LANES), F32)],
        compiler_params=_cparams("parallel", "parallel"),
        name="attn_sample",
    )(q, k, v, cache_k, cache_v, bias, mult)


def _outproj_kernel(h_ref, yc_ref, ya_ref, wo_ref, g_ref, wq_ref, h1_ref, xt_ref, pq_ref, *, conv_ch):
    mix = jnp.dot(yc_ref[...].astype(BF16), wo_ref[0:conv_ch, :], preferred_element_type=F32)
    mix = mix + jnp.dot(ya_ref[...], wo_ref[conv_ch:, :], preferred_element_type=F32)
    h1 = h_ref[...] + mix
    h1_ref[...] = h1
    xn = _rms(h1, g_ref[...])
    pq_ref[...] = jnp.dot(xn.astype(BF16), wq_ref[...], preferred_element_type=F32)
    xt_ref[...] = xn.T.astype(BF16)


def _outproj(h, yc, ya, wo_all, g_all, wq_all, layer):
    rows, d = h.shape
    conv_ch = yc.shape[1]
    nq = wq_all.shape[-1]
    tm = _row_tile(rows, 512)
    row = lambda w: pl.BlockSpec((tm, w), lambda i: (i, 0))
    return pl.pallas_call(
        functools.partial(_outproj_kernel, conv_ch=conv_ch),
        grid=(rows // tm,),
        in_specs=[row(d), row(conv_ch), row(ya.shape[1]),
                  pl.BlockSpec((None, d, d), lambda i: (layer, 0, 0)),
                  pl.BlockSpec((None, 1, d), lambda i: (layer, 0, 0)),
                  pl.BlockSpec((None, d, nq), lambda i: (layer, 0, 0))],
        out_specs=[row(d), pl.BlockSpec((d, tm), lambda i: (0, i)), row(nq)],
        out_shape=[jax.ShapeDtypeStruct((rows, d), F32),
                   jax.ShapeDtypeStruct((d, rows), BF16),
                   jax.ShapeDtypeStruct((rows, nq), F32)],
        compiler_params=_cparams("parallel"),
        name="outproj",
    )(h, yc, ya, wo_all, g_all, wq_all)


def _top16(x, n_rows):
    iota = lax.broadcasted_iota(jnp.int32, x.shape, 0).astype(F32)
    rank = jnp.full(x.shape, float(PEER_TOPK), F32)
    vals = []
    for r in range(PEER_TOPK):
        m = jnp.max(x, axis=0, keepdims=True)
        idx = jnp.min(jnp.where(x == m, iota, float(n_rows)), axis=0, keepdims=True)
        hit = iota == idx
        x = jnp.where(hit, -jnp.inf, x)
        rank = jnp.where(hit, float(r), rank)
        vals.append(m)
    return vals, rank


def _stack_rows(rows):
    n = len(rows)
    iota = lax.broadcasted_iota(jnp.int32, (n, rows[0].shape[1]), 0)
    out = jnp.zeros(iota.shape, F32)
    for r, v in enumerate(rows):
        out = jnp.where(iota == r, v, out)
    return out


def _peer_topk_kernel(pq_ref, k1_ref, k2_ref, lam_ref, e1z_ref, rho_ref, e2_ref):
    half = N_KEYS
    k1 = k1_ref[...].astype(BF16)
    k2 = k2_ref[...].astype(BF16)
    nt = (((1,), (1,)), ((), ()))
    for h in range(PEER_HEADS):
        q1 = pq_ref[:, 2 * h * half:(2 * h + 1) * half].astype(BF16)
        q2 = pq_ref[:, (2 * h + 1) * half:(2 * h + 2) * half].astype(BF16)
        s1 = lax.dot_general(k1, q1, nt, preferred_element_type=F32)
        s2 = lax.dot_general(k2, q2, nt, preferred_element_type=F32)
        t1, rank1 = _top16(s1, N_KEYS)
        t2, rank2 = _top16(s2, N_KEYS)
        t2m = _stack_rows(t2)
        e2r = jnp.exp(t2m - t2[0])
        cand = jnp.concatenate([t1[r] + t2m for r in range(PEER_TOPK)], axis=0)
        prod = jnp.concatenate([jnp.exp(t1[r] - t1[0]) * e2r for r in range(PEER_TOPK)], axis=0)
        _, crank = _top16(cand, PEER_TOPK * PEER_TOPK)
        sel = crank < float(PEER_TOPK)
        z = jnp.sum(jnp.where(sel, prod, 0.0), axis=0, keepdims=True)
        inv_z = 1.0 / z
        self32 = jnp.where(sel, 1.0, 0.0)
        lam = jnp.zeros(s1.shape, F32)
        for r in range(PEER_TOPK):
            cnt = jnp.sum(self32[r * PEER_TOPK:(r + 1) * PEER_TOPK, :], axis=0, keepdims=True)
            lam = jnp.where(rank1 == float(r), cnt, lam)
        lam_ref[h] = lam
        e1z_ref[h] = jnp.exp(s1 - t1[0]) * inv_z
        rho_ref[h] = rank2
        e2_ref[h] = jnp.exp(s2 - t2[0])


def _peer_topk(pq, k1_all, k2_all, layer):
    rows, nq = pq.shape
    tt = LANES
    blk = pl.BlockSpec((PEER_HEADS, N_KEYS, tt), lambda i: (0, 0, i))
    kblk = pl.BlockSpec((None, N_KEYS, N_KEYS), lambda i: (layer, 0, 0))
    return pl.pallas_call(
        _peer_topk_kernel,
        grid=(rows // tt,),
        in_specs=[pl.BlockSpec((tt, nq), lambda i: (i, 0)), kblk, kblk],
        out_specs=[blk] * 4,
        out_shape=[jax.ShapeDtypeStruct((PEER_HEADS, N_KEYS, rows), F32)] * 4,
        compiler_params=_cparams("parallel"),
        name="peer_topk",
    )(pq, k1_all, k2_all)


def _gelu(x):
    return 0.5 * x * (1.0 + lax.erf(x * (2.0 ** -0.5)))


def _peer_main_kernel(xt_ref, u_ref, vt_ref, lam_ref, e1z_ref, rho_ref, e2_ref, h1_ref, o_ref,
                      acc, a_sc, c_sc, *, TT, EB):
    e = pl.program_id(1)

    @pl.when(e == 0)
    def _():
        acc[...] = jnp.zeros(acc.shape, F32)

    a_sc[...] = jnp.dot(u_ref[...], xt_ref[...], preferred_element_type=F32)
    for c in range(EB // N_KEYS):
        rows = slice(c * N_KEYS, (c + 1) * N_KEYS)
        for tc in range(TT // LANES):
            cols = slice(tc * LANES, (tc + 1) * LANES)
            w = jnp.zeros((N_KEYS, LANES), F32)
            for h in range(PEER_HEADS):
                lam = lam_ref[h, c:c + 1, cols]
                e1z = e1z_ref[h, c:c + 1, cols]
                w = w + jnp.where(rho_ref[h, :, cols] < lam, e2_ref[h, :, cols] * e1z, 0.0)
            c_sc[rows, cols] = (w * _gelu(a_sc[rows, cols])).astype(BF16)
    acc[...] += jnp.dot(vt_ref[...], c_sc[...], preferred_element_type=F32)

    @pl.when(e == pl.num_programs(1) - 1)
    def _():
        o_ref[...] = h1_ref[...] + acc[...].T


def _peer_main(xt, u_all, vt_all, lam, e1z, rho, e2, h1, layer):
    d, rows = xt.shape
    n_exp = u_all.shape[1]
    tt = _row_tile(rows, 512)
    eb = 1024
    cb = eb // N_KEYS
    fac_c = pl.BlockSpec((PEER_HEADS, cb, tt), lambda i, e: (0, e, i))
    fac_all = pl.BlockSpec((PEER_HEADS, N_KEYS, tt), lambda i, e: (0, 0, i))
    return pl.pallas_call(
        functools.partial(_peer_main_kernel, TT=tt, EB=eb),
        grid=(rows // tt, n_exp // eb),
        in_specs=[pl.BlockSpec((d, tt), lambda i, e: (0, i)),
                  pl.BlockSpec((None, eb, d), lambda i, e: (layer, e, 0)),
                  pl.BlockSpec((None, d, eb), lambda i, e: (layer, 0, e)),
                  fac_c, fac_c, fac_all, fac_all,
                  pl.BlockSpec((tt, d), lambda i, e: (i, 0))],
        out_specs=pl.BlockSpec((tt, d), lambda i, e: (i, 0)),
        out_shape=jax.ShapeDtypeStruct((rows, d), F32),
        scratch_shapes=[pltpu.VMEM((d, tt), F32), pltpu.VMEM((eb, tt), F32), pltpu.VMEM((eb, tt), BF16)],
        compiler_params=_cparams("parallel", "arbitrary"),
        name="peer_main",
    )(xt, u_all, vt_all, lam, e1z, rho, e2, h1)


def _ple_kernel(h_ref, p_ref, g_ref, wg_ref, wp_ref, gf_ref, o_ref, *, final):
    h = h_ref[...]
    gate = jax.nn.sigmoid(jnp.dot(_rms(h, g_ref[...]).astype(BF16), wg_ref[...], preferred_element_type=F32))
    h = h + jnp.dot(p_ref[...].astype(BF16), wp_ref[...], preferred_element_type=F32) * gate
    o_ref[...] = _rms(h, gf_ref[...]) if final else h


def _ple(h, p_all, g_all, wg_all, wp_all, g_final, layer, final):
    rows, d = h.shape
    pd = p_all.shape[-1]
    tm = _row_tile(rows, 512)
    return pl.pallas_call(
        functools.partial(_ple_kernel, final=final),
        grid=(rows // tm,),
        in_specs=[pl.BlockSpec((tm, d), lambda i: (i, 0)),
                  pl.BlockSpec((None, tm, pd), lambda i: (layer, i, 0)),
                  pl.BlockSpec((None, 1, d), lambda i: (layer, 0, 0)),
                  pl.BlockSpec((None, d, d), lambda i: (layer, 0, 0)),
                  pl.BlockSpec((None, pd, d), lambda i: (layer, 0, 0)),
                  pl.BlockSpec((1, d), lambda i: (0, 0))],
        out_specs=pl.BlockSpec((tm, d), lambda i: (i, 0)),
        out_shape=jax.ShapeDtypeStruct((rows, d), F32),
        compiler_params=_cparams("parallel"),
        name="ple",
    )(h, p_all, g_all, wg_all, wp_all, g_final)


def kernel(x_prompt, x_sample, cache_k, cache_v, state_conv, p_prompt, p_sample, g_mix, w_in, conv_w, w_out,
           rel_bias, g_ffn, peer_wq, peer_k1, peer_k2, peer_u, peer_v, g_ple, w_ple_gate, w_ple_proj, g_final):
    B, T, D = x_prompt.shape
    S, TS, _ = x_sample.shape
    depth = w_in.shape[0]
    conv_ch = conv_w.shape[-1]
    attn_w = D - conv_ch
    n_heads = attn_w // HEAD_DIM
    L = cache_k.shape[2]
    keep = min(MAX_WINDOW, T)
    QB = 128
    assert all(w == BAND * d for w, d in DILATED_PATTERNS)
    assert T % (QB * max(d for _, d in DILATED_PATTERNS)) == 0 and attn_w % LANES == 0

    w_in_b = w_in.astype(BF16)
    w_out_b = w_out.astype(BF16)
    wq_b = peer_wq.astype(BF16)
    wg_b = w_ple_gate.astype(BF16)
    wp_b = w_ple_proj.astype(BF16)
    u_b = peer_u.astype(BF16)
    vt_b = jnp.swapaxes(peer_v, 1, 2).astype(BF16)
    g_mix3, g_ffn3, g_ple3 = (g.reshape(depth, 1, D) for g in (g_mix, g_ffn, g_ple))
    g_fin2 = g_final.reshape(1, D)
    ck = cache_k.reshape(depth, S, L, attn_w)
    cv = cache_v.reshape(depth, S, L, attn_w)
    pp = p_prompt.reshape(depth, B * T, -1)
    ps = p_sample.reshape(depth, S * TS, -1)

    bias_tab = _prompt_bias_tables(rel_bias, QB)
    LP = L + LANES
    sbias, smult = _sample_tables(rel_bias, L, TS, LP)

    hp = x_prompt.reshape(B * T, D)
    hs = x_sample.reshape(S * TS, D)
    zero_past = jnp.zeros((B, 2, conv_ch), F32)
    kp_l, vp_l, cp_l, ks_l, vs_l, cs_l = [], [], [], [], [], []

    def mixer_tail(h, yc, ya, lam_src_layer):
        i = lam_src_layer
        h1, xt, pq = _outproj(h, yc, ya, w_out_b, g_ffn3, wq_b, i)
        lam, e1z, rho, e2 = _peer_topk(pq, peer_k1, peer_k2, i)
        h2 = _peer_main(xt, u_b, vt_b, lam, e1z, rho, e2, h1, i)
        return h2

    for i in range(depth):
        final = i == depth - 1
        u, bg, q, k, v = _inproj(hp, g_mix3, w_in_b, i, conv_ch, attn_w)
        yc, cst = _conv(u, bg, zero_past, conv_w, i, B, T)
        ya = _attn_prompt(q, k, v, bias_tab, B, T, QB)
        h2 = mixer_tail(hp, yc, ya, i)
        hp = _ple(h2, pp, g_ple3, wg_b, wp_b, g_fin2, i, final)
        kp_l.append(k.reshape(B, T, n_heads, HEAD_DIM)[:, T - keep:])
        vp_l.append(v.reshape(B, T, n_heads, HEAD_DIM)[:, T - keep:])
        cp_l.append(cst)
        u, bg, q, k, v = _inproj(hs, g_mix3, w_in_b, i, conv_ch, attn_w)
        yc, cst = _conv(u, bg, state_conv[i], conv_w, i, S, TS)
        ya = _attn_sample(q, k, v, ck, cv, sbias, smult, i, S, TS)
        h2 = mixer_tail(hs, yc, ya, i)
        hs = _ple(h2, ps, g_ple3, wg_b, wp_b, g_fin2, i, final)
        ks_l.append(k.reshape(S, TS, n_heads, HEAD_DIM))
        vs_l.append(v.reshape(S, TS, n_heads, HEAD_DIM))
        cs_l.append(cst)

    return (hp.reshape(B, T, D), hs.reshape(S, TS, D), jnp.stack(kp_l), jnp.stack(vp_l), jnp.stack(cp_l),
            jnp.stack(ks_l), jnp.stack(vs_l), jnp.stack(cs_l))
```

```python
import functools
import math

import numpy as np
import jax
import jax.numpy as jnp
from jax import lax
from jax.experimental import pallas as pl
from jax.experimental.pallas import tpu as pltpu

F32 = jnp.float32
BF16 = jnp.bfloat16

HEAD_DIM = 64
LANES = 128
DILATED_PATTERNS = ((128, 1), (512, 4), (2048, 16))
BAND = 128
MAX_WINDOW = 2048
N_BUCKETS = 32
MAX_DISTANCE = 2048
N_KEYS = 128
PEER_HEADS = 8
PEER_TOPK = 16
EPS = 1e-6
NEG = -1e30
VMEM_LIMIT_BYTES = 56 * 1024 * 1024


def _cparams(*sem):
    return pltpu.CompilerParams(dimension_semantics=sem, vmem_limit_bytes=VMEM_LIMIT_BYTES)


def _row_tile(rows, cap):
    t = min(rows, cap)
    while rows % t:
        t //= 2
    return t


def _rms(x, g):
    return x * lax.rsqrt(jnp.mean(x * x, axis=-1, keepdims=True) + EPS) * g


def _inproj_kernel(x_ref, g_ref, w_ref, u_ref, bg_ref, q_ref, k_ref, v_ref, *, conv_ch, attn_w):
    xn = _rms(x_ref[...], g_ref[...])
    proj = jnp.dot(xn.astype(BF16), w_ref[...], preferred_element_type=F32)
    c, a = conv_ch, attn_w
    u_ref[...] = proj[:, 2 * c:3 * c] * proj[:, :c]
    bg_ref[...] = proj[:, c:2 * c]
    q_ref[...] = proj[:, 3 * c:3 * c + a]
    k_ref[...] = proj[:, 3 * c + a:3 * c + 2 * a]
    v_ref[...] = proj[:, 3 * c + 2 * a:]


def _inproj(x, g_all, w_all, layer, conv_ch, attn_w):
    rows, d = x.shape
    tm = _row_tile(rows, 512)
    n_out = w_all.shape[-1]
    row = lambda w: pl.BlockSpec((tm, w), lambda i: (i, 0))
    return pl.pallas_call(
        functools.partial(_inproj_kernel, conv_ch=conv_ch, attn_w=attn_w),
        grid=(rows // tm,),
        in_specs=[row(d),
                  pl.BlockSpec((None, 1, d), lambda i: (layer, 0, 0)),
                  pl.BlockSpec((None, d, n_out), lambda i: (layer, 0, 0))],
        out_specs=[row(conv_ch), row(conv_ch), row(attn_w), row(attn_w), row(attn_w)],
        out_shape=[jax.ShapeDtypeStruct((rows, conv_ch), F32)] * 2
        + [jax.ShapeDtypeStruct((rows, attn_w), F32)] * 3,
        compiler_params=_cparams("parallel"),
        name="inproj",
    )(x, g_all, w_all)


def _conv_kernel(u_ref, bg_ref, past_ref, w_ref, y_ref, st_ref, *, T):
    u = u_ref[...]
    row = lax.broadcasted_iota(jnp.int32, u.shape, 0)
    p0 = past_ref[0:1, :]
    p1 = past_ref[1:2, :]
    um1 = jnp.where(row == 0, p1, pltpu.roll(u, 1, 0))
    um2 = jnp.where(row == 0, p0, jnp.where(row == 1, p1, pltpu.roll(u, 2, 0)))
    conv = w_ref[0:1, :] * um2 + w_ref[1:2, :] * um1 + w_ref[2:3, :] * u
    y_ref[...] = bg_ref[...] * conv
    st_ref[...] = u_ref[T - 2:T, :]


def _conv(u, bg, past, w_all, layer, nseq, T):
    rows, c = u.shape
    blk = pl.BlockSpec((T, c), lambda s: (s, 0))
    return pl.pallas_call(
        functools.partial(_conv_kernel, T=T),
        grid=(nseq,),
        in_specs=[blk, blk,
                  pl.BlockSpec((None, 2, c), lambda s: (s, 0, 0)),
                  pl.BlockSpec((None, 3, c), lambda s: (layer, 0, 0))],
        out_specs=[blk, pl.BlockSpec((None, 2, c), lambda s: (s, 0, 0))],
        out_shape=[jax.ShapeDtypeStruct((rows, c), F32),
                   jax.ShapeDtypeStruct((nseq, 2, c), F32)],
        compiler_params=_cparams("parallel"),
        name="conv",
    )(u, bg, past, w_all)


def _t5_bucket_np(dist):
    max_exact = N_BUCKETS // 2
    df = np.maximum(dist, 1).astype(np.float32)
    large = max_exact + (np.log(df / np.float32(max_exact)) / np.float32(math.log(MAX_DISTANCE / max_exact))
                         * np.float32(N_BUCKETS - max_exact)).astype(np.int32)
    large = np.minimum(large, N_BUCKETS - 1)
    return np.where(dist < max_exact, dist, large).astype(np.int32)


def _static_rows(table, idx):
    onehot = jnp.asarray(np.eye(table.shape[0], dtype=np.float32)[idx])
    return jnp.einsum('...k,kh->...h', onehot, table, precision=lax.Precision.HIGHEST)


def _prompt_bias_rows(rel_bias, kb):
    m = np.arange(kb)
    idx = np.full((len(DILATED_PATTERNS), kb), N_BUCKETS, np.int32)
    for r, (_, dil) in enumerate(DILATED_PATTERNS):
        idx[r] = np.where(m <= BAND, _t5_bucket_np(np.clip(BAND - m, 0, BAND) * dil), N_BUCKETS)
    ext = jnp.concatenate([rel_bias.astype(F32), jnp.full((1, rel_bias.shape[1]), NEG, F32)], axis=0)
    rows = jnp.transpose(_static_rows(ext, idx), (0, 2, 1))
    return rows.reshape(rows.shape[0], rows.shape[1] // 2, 2, kb)


def _sample_tables(rel_bias, L, T, lp):
    t = np.arange(T)[:, None]
    pos = np.arange(lp)[None, :]
    dist = L + t - pos
    valid = (dist >= 0) & (pos < L + T)
    mult = np.zeros((T, lp), np.float32)
    for window, dil in DILATED_PATTERNS:
        mult += (valid & (dist % dil == 0) & (dist <= window)).astype(np.float32)
    idx = _t5_bucket_np(np.clip(dist, 0, MAX_DISTANCE))
    bias = jnp.transpose(_static_rows(rel_bias.astype(F32), idx), (2, 0, 1))
    return bias, jnp.asarray(mult)


MERGE_ROWS = 256


def _attn_prompt_kernel(q_ref, k_ref, v_ref, brow_ref, o_ref, kp, vp, tab, m_sc, l_sc, acc_sc, *, T, QB, PAD):
    KB = QB + BAND
    npat = len(DILATED_PATTERNS)
    zeros = jnp.zeros((PAD, LANES), F32)
    kp[0:PAD, :] = zeros
    vp[0:PAD, :] = zeros
    kp[PAD:PAD + T, :] = k_ref[...]
    vp[PAD:PAD + T, :] = v_ref[...]
    before_start = lax.broadcasted_iota(jnp.int32, (QB, KB), 1) < BAND
    for r in range(npat):
        for hh in range(2):
            row = jnp.broadcast_to(brow_ref[r, hh:hh + 1, :], (QB, KB))
            toe = pltpu.roll(row, 0, 1, stride=1, stride_axis=0)
            tab[r, 0, hh * QB:(hh + 1) * QB, :] = toe
            tab[r, 1, hh * QB:(hh + 1) * QB, :] = jnp.where(before_start, NEG, toe)
    is0 = lax.broadcasted_iota(jnp.int32, (1, LANES), 1) < HEAD_DIM
    scale = HEAD_DIM ** -0.5

    for r, (_, d) in enumerate(DILATED_PATTERNS):
        nblk = (T // d) // QB

        def body(idx, carry, r=r, d=d, nblk=nblk):
            c = idx // nblk
            bi = idx % nblk
            a0 = bi * QB
            qstart = c + d * a0
            kstart = PAD + c + d * (a0 - BAND)
            if d == 1:
                qsl = pl.ds(pl.multiple_of(qstart, QB), QB)
                ksl = pl.ds(pl.multiple_of(kstart, BAND), KB)
            else:
                qsl = pl.ds(qstart, QB, stride=d)
                ksl = pl.ds(kstart, KB, stride=d)
            q = q_ref[qsl, :] * scale
            kb = kp[ksl, :].astype(BF16)
            vb = vp[ksl, :].astype(BF16)
            variant = jnp.where(bi == 0, 1, 0)
            q2 = jnp.concatenate([jnp.where(is0, q, 0.0), jnp.where(is0, 0.0, q)], axis=0).astype(BF16)
            sc = lax.dot_general(q2, kb, (((1,), (1,)), ((), ())), preferred_element_type=F32)
            sc = sc + tab[r, variant]
            m = jnp.max(sc, axis=1, keepdims=True)
            p = jnp.exp(sc - m)
            l = jnp.sum(p, axis=1, keepdims=True)
            pv = jnp.dot(p.astype(BF16), vb, preferred_element_type=F32)
            m_sc[r, qsl, :] = jnp.where(is0, m[:QB], m[QB:])
            l_sc[r, qsl, :] = jnp.where(is0, l[:QB], l[QB:])
            acc_sc[r, qsl, :] = jnp.where(is0, pv[:QB], pv[QB:])
            return carry

        lax.fori_loop(0, d * nblk, body, 0, unroll=16)

    def merge(i, carry):
        rows = pl.ds(pl.multiple_of(i * MERGE_ROWS, MERGE_ROWS), MERGE_ROWS)
        m = [m_sc[r, rows, :] for r in range(npat)]
        mx = functools.reduce(jnp.maximum, m)
        w = [jnp.exp(mr - mx) for mr in m]
        num = sum(w[r] * acc_sc[r, rows, :] for r in range(npat))
        den = sum(w[r] * l_sc[r, rows, :] for r in range(npat))
        o_ref[rows, :] = (num / den).astype(o_ref.dtype)
        return carry

    lax.fori_loop(0, T // MERGE_ROWS, merge, 0)


def _attn_prompt(q, k, v, bias_rows, B, T, QB):
    rows, aw = q.shape
    npair = aw // LANES
    npat = len(DILATED_PATTERNS)
    PAD = BAND * max(d for _, d in DILATED_PATTERNS)
    KB = QB + BAND
    blk = pl.BlockSpec((T, LANES), lambda b, hp: (b, hp))
    return pl.pallas_call(
        functools.partial(_attn_prompt_kernel, T=T, QB=QB, PAD=PAD),
        grid=(B, npair),
        in_specs=[blk, blk, blk,
                  pl.BlockSpec((npat, None, 2, KB), lambda b, hp: (0, hp, 0, 0))],
        out_specs=blk,
        out_shape=jax.ShapeDtypeStruct((rows, aw), BF16),
        scratch_shapes=[pltpu.VMEM((PAD + T, LANES), F32), pltpu.VMEM((PAD + T, LANES), F32),
                        pltpu.VMEM((npat, 2, 2 * QB, KB), F32),
                        pltpu.VMEM((npat, T, LANES), F32), pltpu.VMEM((npat, T, LANES), F32),
                        pltpu.VMEM((npat, T, LANES), F32)],
        compiler_params=_cparams("parallel", "parallel"),
        name="attn_prompt",
    )(q, k, v, bias_rows)


def _attn_sample_kernel(q_ref, k_ref, v_ref, ck_ref, cv_ref, bias_ref, mult_ref, o_ref, kf, vf, *, L, T, LP):
    kf[0:L, :] = ck_ref[...]
    vf[0:L, :] = cv_ref[...]
    kf[L:LP, :] = jnp.zeros((LP - L, LANES), F32)
    vf[L:LP, :] = jnp.zeros((LP - L, LANES), F32)
    kf[L:L + T, :] = k_ref[...]
    vf[L:L + T, :] = v_ref[...]
    is0 = lax.broadcasted_iota(jnp.int32, (1, LANES), 1) < HEAD_DIM
    q = q_ref[...] * (HEAD_DIM ** -0.5)
    kb = kf[...].astype(BF16)
    vb = vf[...].astype(BF16)
    mult = mult_ref[...]
    outs = []
    for hh in range(2):
        qh = jnp.where(is0 if hh == 0 else jnp.logical_not(is0), q, 0.0).astype(BF16)
        s = lax.dot_general(qh, kb, (((1,), (1,)), ((), ())), preferred_element_type=F32)
        s = jnp.where(mult > 0.0, s + bias_ref[hh], NEG)
        m = jnp.max(s, axis=1, keepdims=True)
        p = jnp.exp(s - m) * mult
        l = jnp.sum(p, axis=1, keepdims=True)
        outs.append(jnp.dot(p.astype(BF16), vb, preferred_element_type=F32) / l)
    o_ref[...] = jnp.where(is0, outs[0], outs[1]).astype(o_ref.dtype)


def _attn_sample(q, k, v, cache_k, cache_v, bias, mult, layer, S, T):
    rows, aw = q.shape
    npair = aw // LANES
    L = cache_k.shape[2]
    LP = bias.shape[-1]
    new = pl.BlockSpec((T, LANES), lambda s, hp: (s, hp))
    cache = pl.BlockSpec((None, None, L, LANES), lambda s, hp: (layer, s, 0, hp))
    return pl.pallas_call(
        functools.partial(_attn_sample_kernel, L=L, T=T, LP=LP),
        grid=(S, npair),
        in_specs=[new, new, new, cache, cache,
                  pl.BlockSpec((2, T, LP), lambda s, hp: (hp, 0, 0)),
                  pl.BlockSpec((T, LP), lambda s, hp: (0, 0))],
        out_specs=new,
        out_shape=jax.ShapeDtypeStruct((rows, aw), BF16),
        scratch_shapes=[pltpu.VMEM((LP, LANES), F32), pltpu.VMEM((LP, LANES), F32)],
        compiler_params=_cparams("parallel", "parallel"),
        name="attn_sample",
    )(q, k, v, cache_k, cache_v, bias, mult)


def _outproj_kernel(h_ref, yc_ref, ya_ref, wo_ref, g_ref, wq_ref, h1_ref, xt_ref, pq_ref, *, conv_ch):
    mix = jnp.dot(yc_ref[...].astype(BF16), wo_ref[0:conv_ch, :], preferred_element_type=F32)
    mix = mix + jnp.dot(ya_ref[...], wo_ref[conv_ch:, :], preferred_element_type=F32)
    h1 = h_ref[...] + mix
    h1_ref[...] = h1
    xn = _rms(h1, g_ref[...])
    pq_ref[...] = jnp.dot(xn.astype(BF16), wq_ref[...], preferred_element_type=F32)
    xt_ref[...] = pltpu.bitcast(xn.T.astype(BF16), jnp.uint32)


def _outproj(h, yc, ya, wo_all, g_all, wq_all, layer):
    rows, d = h.shape
    conv_ch = yc.shape[1]
    nq = wq_all.shape[-1]
    tm = _row_tile(rows, 512)
    row = lambda w: pl.BlockSpec((tm, w), lambda i: (i, 0))
    return pl.pallas_call(
        functools.partial(_outproj_kernel, conv_ch=conv_ch),
        grid=(rows // tm,),
        in_specs=[row(d), row(conv_ch), row(ya.shape[1]),
                  pl.BlockSpec((None, d, d), lambda i: (layer, 0, 0)),
                  pl.BlockSpec((None, 1, d), lambda i: (layer, 0, 0)),
                  pl.BlockSpec((None, d, nq), lambda i: (layer, 0, 0))],
        out_specs=[row(d), pl.BlockSpec((d // 2, tm), lambda i: (0, i)), row(nq)],
        out_shape=[jax.ShapeDtypeStruct((rows, d), F32),
                   jax.ShapeDtypeStruct((d // 2, rows), jnp.uint32),
                   jax.ShapeDtypeStruct((rows, nq), F32)],
        compiler_params=_cparams("parallel"),
        name="outproj",
    )(h, yc, ya, wo_all, g_all, wq_all)


def _top16_exact(x, pos):
    big = float(PEER_TOPK * PEER_TOPK)
    rank = jnp.full(x.shape, float(PEER_TOPK), F32)
    vals = []
    for r in range(PEER_TOPK):
        m = jnp.max(x, axis=0, keepdims=True)
        idx = jnp.min(jnp.where(x == m, pos, big), axis=0, keepdims=True)
        hit = pos == idx
        x = jnp.where(hit, -jnp.inf, x)
        rank = jnp.where(hit, float(r), rank)
        vals.append(m)
    return vals, rank


def _top16_distinct(x):
    rank = jnp.full(x.shape, float(PEER_TOPK), F32)
    vals = []
    for r in range(PEER_TOPK):
        m = jnp.max(x, axis=0, keepdims=True)
        hit = x == m
        x = jnp.where(hit, -jnp.inf, x)
        rank = jnp.where(hit, float(r), rank)
        vals.append(m)
    count = jnp.sum(jnp.where(rank < float(PEER_TOPK), 1.0, 0.0), axis=0, keepdims=True)
    return vals, rank, count


def _stack_rows(rows):
    n = len(rows)
    iota = lax.broadcasted_iota(jnp.int32, (n, rows[0].shape[1]), 0)
    out = jnp.zeros(iota.shape, F32)
    for r, v in enumerate(rows):
        out = jnp.where(iota == r, v, out)
    return out


_CAND_BLOCKS = ((0, 16), (1, 16), (2, 8), (3, 8), (4, 8), (5, 8), (6, 8), (7, 8))
_CAND_TAIL_R1 = 8


def _cand_pos():
    pos = []
    for r1, n2 in _CAND_BLOCKS:
        pos += [r1 * PEER_TOPK + r2 for r2 in range(n2)]
    pos += [r1 * PEER_TOPK for r1 in range(_CAND_TAIL_R1, PEER_TOPK)]
    return np.asarray(pos, np.float32)[:, None]


def _cand_build(a_rows, a_mat, b_mat, b_row0, op):
    parts = [op(a_rows[r1], b_mat[0:n2, :]) for r1, n2 in _CAND_BLOCKS]
    parts.append(op(a_mat[_CAND_TAIL_R1:, :], b_row0))
    return jnp.concatenate(parts, axis=0)


def _cand_counts(member):
    counts, off = [], 0
    for _, n2 in _CAND_BLOCKS:
        counts.append(jnp.sum(member[off:off + n2, :], axis=0, keepdims=True))
        off += n2
    for k in range(PEER_TOPK - _CAND_TAIL_R1):
        counts.append(member[off + k:off + k + 1, :])
    return counts


def _dup_bf16_words(x):
    b = pltpu.bitcast(x.astype(BF16).astype(F32), jnp.uint32)
    return b | (b >> 16)


def _peer_head_factors(s1, s2, pos_c, exact):
    if exact:
        key_pos = lax.broadcasted_iota(jnp.int32, s1.shape, 0).astype(F32)
        t1, rank1 = _top16_exact(s1, key_pos)
        t2, rank2 = _top16_exact(s2, key_pos)
        ok = None
    else:
        t1, rank1, n1 = _top16_distinct(s1)
        t2, rank2, n2 = _top16_distinct(s2)
    t1m, t2m = _stack_rows(t1), _stack_rows(t2)
    e1r = jnp.exp(t1m - t1[0])
    e2r = jnp.exp(t2m - t2[0])
    cand = _cand_build(t1, t1m, t2m, t2[0], lambda a, b: a + b)
    prod = _cand_build([e1r[r:r + 1, :] for r in range(PEER_TOPK)], e1r, e2r, e2r[0:1, :], lambda a, b: a * b)
    if exact:
        _, crank = _top16_exact(cand, pos_c)
        sel = crank < float(PEER_TOPK)
    else:
        x = cand
        for _ in range(PEER_TOPK):
            x = jnp.where(x == jnp.max(x, axis=0, keepdims=True), -jnp.inf, x)
        sel = x == -jnp.inf
    member = jnp.where(sel, 1.0, 0.0)
    if not exact:
        nc = jnp.sum(member, axis=0, keepdims=True)
        full = float(PEER_TOPK)
        ok = jnp.logical_and(jnp.logical_and(n1 == full, n2 == full), nc == full)
    z = jnp.sum(member * prod, axis=0, keepdims=True)
    counts = _cand_counts(member)
    lam = jnp.zeros(s1.shape, F32)
    for r in range(PEER_TOPK):
        lam = jnp.where(rank1 == float(r), counts[r], lam)
    e1z = jnp.exp(s1 - t1[0]) * (1.0 / z)
    e2 = jnp.exp(s2 - t2[0])
    return (lam, e1z, rank2, e2), ok


def _peer_topk_kernel(pq_ref, k1_ref, k2_ref, pos_ref, lam_ref, e1z_ref, rho_ref, e2_ref):
    k1 = k1_ref[...].astype(BF16)
    k2 = k2_ref[...].astype(BF16)
    pos_c = pos_ref[...]
    nt = (((1,), (1,)), ((), ()))

    def scores(h):
        q1 = pq_ref[:, 2 * h * N_KEYS:(2 * h + 1) * N_KEYS].astype(BF16)
        q2 = pq_ref[:, (2 * h + 1) * N_KEYS:(2 * h + 2) * N_KEYS].astype(BF16)
        return (lax.dot_general(k1, q1, nt, preferred_element_type=F32),
                lax.dot_general(k2, q2, nt, preferred_element_type=F32))

    def emit(h, fac):
        lam, e1z, rho, e2 = fac
        lam_ref[h] = _dup_bf16_words(lam)
        e1z_ref[h] = _dup_bf16_words(e1z)
        rho_ref[h] = pltpu.bitcast(rho.astype(BF16), jnp.uint32)
        e2_ref[h] = pltpu.bitcast(e2.astype(BF16), jnp.uint32)

    bad = jnp.zeros((1, pq_ref.shape[0]), F32)
    for h in range(PEER_HEADS):
        fac, ok = _peer_head_factors(*scores(h), pos_c, exact=False)
        emit(h, fac)
        bad = jnp.maximum(bad, jnp.where(ok, 0.0, 1.0))

    @pl.when(jnp.max(bad) > 0.0)
    def _():
        for h in range(PEER_HEADS):
            fac, _ = _peer_head_factors(*scores(h), pos_c, exact=True)
            emit(h, fac)


def _peer_topk(pq, k1_all, k2_all, layer):
    rows, nq = pq.shape
    tt = LANES
    pos_c = jnp.asarray(np.broadcast_to(_cand_pos(), (_cand_pos().shape[0], tt)))
    blk = pl.BlockSpec((PEER_HEADS, N_KEYS, tt), lambda i: (0, 0, i))
    hblk = pl.BlockSpec((PEER_HEADS, N_KEYS // 2, tt), lambda i: (0, 0, i))
    kblk = pl.BlockSpec((None, N_KEYS, N_KEYS), lambda i: (layer, 0, 0))
    word = jax.ShapeDtypeStruct((PEER_HEADS, N_KEYS, rows), jnp.uint32)
    half = jax.ShapeDtypeStruct((PEER_HEADS, N_KEYS // 2, rows), jnp.uint32)
    return pl.pallas_call(
        _peer_topk_kernel,
        grid=(rows // tt,),
        in_specs=[pl.BlockSpec((tt, nq), lambda i: (i, 0)), kblk, kblk,
                  pl.BlockSpec(pos_c.shape, lambda i: (0, 0))],
        out_specs=[blk, blk, hblk, hblk],
        out_shape=[word, word, half, half],
        compiler_params=_cparams("parallel"),
        name="peer_topk",
    )(pq, k1_all, k2_all, pos_c)


def _gelu(x):
    return 0.5 * x * (1.0 + lax.erf(x * (2.0 ** -0.5)))


def _pack_rows(x):
    return pltpu.bitcast(x.astype(BF16), jnp.uint32)


def _unpack_rows(w):
    return pltpu.bitcast(w, BF16)


def _vt_prep_kernel(v_ref, o_ref):
    o_ref[...] = _pack_rows(v_ref[...].T)


def _u_prep_kernel(u_ref, o_ref):
    o_ref[...] = _pack_rows(u_ref[...])


def _expert_prep(peer_u, peer_v):
    depth, n_exp, d = peer_u.shape
    eb = 1024
    src = pl.BlockSpec((None, eb, d), lambda l, e: (l, e, 0))
    u_w = pl.pallas_call(
        _u_prep_kernel,
        grid=(depth, n_exp // eb),
        in_specs=[src],
        out_specs=pl.BlockSpec((None, eb // 2, d), lambda l, e: (l, e, 0)),
        out_shape=jax.ShapeDtypeStruct((depth, n_exp // 2, d), jnp.uint32),
        compiler_params=_cparams("parallel", "parallel"),
        name="u_prep",
    )(peer_u)
    vt_w = pl.pallas_call(
        _vt_prep_kernel,
        grid=(depth, n_exp // eb),
        in_specs=[src],
        out_specs=pl.BlockSpec((None, d // 2, eb), lambda l, e: (l, 0, e)),
        out_shape=jax.ShapeDtypeStruct((depth, d // 2, n_exp), jnp.uint32),
        compiler_params=_cparams("parallel", "parallel"),
        name="vt_prep",
    )(peer_v)
    return u_w, vt_w


PACKED_ROWS = 16


def _peer_main_kernel(xt_ref, u_ref, vt_ref, lam_ref, e1z_ref, rho_ref, e2_ref, h1_ref, o_ref,
                      acc, a_even, a_odd, c_even, c_odd, *, TT, EB, NE, NS):
    s = pl.program_id(0)
    j2 = jnp.clip(s - 2, 0, NS - 1)
    e_j2 = lax.rem(j2, NE)
    parity = lax.rem(s, 2)
    half = PACKED_ROWS // 2

    @pl.when(s == 0)
    def _():
        a_odd[...] = jnp.zeros(a_odd.shape, F32)
        c_even[...] = jnp.zeros(c_even.shape, jnp.uint32)

    @pl.when(e_j2 == 0)
    def _():
        acc[...] = jnp.zeros(acc.shape, F32)

    def stages(a_cur, a_prev, c_cur, c_prev):
        cb = _unpack_rows(c_cur[...])
        xt = _unpack_rows(xt_ref[...])
        d_slab = acc.shape[0] // (EB // N_KEYS)
        for c in range(EB // N_KEYS):
            srows = slice(c * N_KEYS, (c + 1) * N_KEYS)
            u_slab = _unpack_rows(u_ref[c * N_KEYS // 2:(c + 1) * N_KEYS // 2, :])
            a_cur[srows, :] = jnp.dot(u_slab, xt, preferred_element_type=F32)
            for tc in range(TT // LANES):
                cols = slice(tc * LANES, (tc + 1) * LANES)
                bcast = lambda ref, h: pltpu.bitcast(jnp.broadcast_to(ref[h, c:c + 1, cols], (half, LANES)), BF16)
                lam = [bcast(lam_ref, h) for h in range(PEER_HEADS)]
                e1z = [bcast(e1z_ref, h) for h in range(PEER_HEADS)]
                for jv in range(N_KEYS // PACKED_ROWS):
                    wrows = slice(jv * half, (jv + 1) * half)
                    w = jnp.zeros((PACKED_ROWS, LANES), BF16)
                    for h in range(PEER_HEADS):
                        rho = pltpu.bitcast(rho_ref[h, wrows, cols], BF16)
                        e2 = pltpu.bitcast(e2_ref[h, wrows, cols], BF16)
                        w = w + jnp.where(rho < lam[h], e2 * e1z[h], 0.0)
                    e0 = c * N_KEYS + jv * PACKED_ROWS
                    act = _gelu(a_prev[e0:e0 + PACKED_ROWS, cols]).astype(BF16)
                    c_prev[e0 // 2:e0 // 2 + half, cols] = pltpu.bitcast(w * act, jnp.uint32)
            drows = slice(c * d_slab, (c + 1) * d_slab)
            vt_slab = _unpack_rows(vt_ref[c * d_slab // 2:(c + 1) * d_slab // 2, :])
            acc[drows, :] += jnp.dot(vt_slab, cb, preferred_element_type=F32)

    @pl.when(parity == 0)
    def _():
        stages(a_even, a_odd, c_even, c_odd)

    @pl.when(parity == 1)
    def _():
        stages(a_odd, a_even, c_odd, c_even)

    @pl.when(jnp.logical_and(e_j2 == NE - 1, s >= 2))
    def _():
        o_ref[...] = h1_ref[...] + acc[...].T


def _peer_main(xt, u_all, vt_all, lam, e1z, rho, e2, h1, layer):
    rows, d = h1.shape
    n_exp = vt_all.shape[2]
    tt = _row_tile(rows, 512)
    eb = 1024
    cb = eb // N_KEYS
    ne = n_exp // eb
    ns = (rows // tt) * ne
    j0 = lambda s: jnp.minimum(s, ns - 1)
    j1 = lambda s: jnp.clip(s - 1, 0, ns - 1)
    j2 = lambda s: jnp.clip(s - 2, 0, ns - 1)
    fac_c = pl.BlockSpec((PEER_HEADS, cb, tt), lambda s: (0, j1(s) % ne, j1(s) // ne))
    fac_all = pl.BlockSpec((PEER_HEADS, N_KEYS // 2, tt), lambda s: (0, 0, j1(s) // ne))
    return pl.pallas_call(
        functools.partial(_peer_main_kernel, TT=tt, EB=eb, NE=ne, NS=ns),
        grid=(ns + 2,),
        in_specs=[pl.BlockSpec((d // 2, tt), lambda s: (0, j0(s) // ne)),
                  pl.BlockSpec((None, eb // 2, d), lambda s: (layer, j0(s) % ne, 0)),
                  pl.BlockSpec((None, d // 2, eb), lambda s: (layer, 0, j2(s) % ne)),
                  fac_c, fac_c, fac_all, fac_all,
                  pl.BlockSpec((tt, d), lambda s: (j2(s) // ne, 0))],
        out_specs=pl.BlockSpec((tt, d), lambda s: (j2(s) // ne, 0)),
        out_shape=jax.ShapeDtypeStruct((rows, d), F32),
        scratch_shapes=[pltpu.VMEM((d, tt), F32), pltpu.VMEM((eb, tt), F32), pltpu.VMEM((eb, tt), F32),
                        pltpu.VMEM((eb // 2, tt), jnp.uint32), pltpu.VMEM((eb // 2, tt), jnp.uint32)],
        compiler_params=_cparams("arbitrary"),
        name="peer_main",
    )(xt, u_all, vt_all, lam, e1z, rho, e2, h1)


def _ple_kernel(h_ref, p_ref, g_ref, wg_ref, wp_ref, gf_ref, o_ref, *, final):
    h = h_ref[...]
    gate = jax.nn.sigmoid(jnp.dot(_rms(h, g_ref[...]).astype(BF16), wg_ref[...], preferred_element_type=F32))
    h = h + jnp.dot(p_ref[...].astype(BF16), wp_ref[...], preferred_element_type=F32) * gate
    o_ref[...] = _rms(h, gf_ref[...]) if final else h


def _ple(h, p_all, g_all, wg_all, wp_all, g_final, layer, final):
    rows, d = h.shape
    pd = p_all.shape[-1]
    tm = _row_tile(rows, 512)
    return pl.pallas_call(
        functools.partial(_ple_kernel, final=final),
        grid=(rows // tm,),
        in_specs=[pl.BlockSpec((tm, d), lambda i: (i, 0)),
                  pl.BlockSpec((None, tm, pd), lambda i: (layer, i, 0)),
                  pl.BlockSpec((None, 1, d), lambda i: (layer, 0, 0)),
                  pl.BlockSpec((None, d, d), lambda i: (layer, 0, 0)),
                  pl.BlockSpec((None, pd, d), lambda i: (layer, 0, 0)),
                  pl.BlockSpec((1, d), lambda i: (0, 0))],
        out_specs=pl.BlockSpec((tm, d), lambda i: (i, 0)),
        out_shape=jax.ShapeDtypeStruct((rows, d), F32),
        compiler_params=_cparams("parallel"),
        name="ple",
    )(h, p_all, g_all, wg_all, wp_all, g_final)


def kernel(x_prompt, x_sample, cache_k, cache_v, state_conv, p_prompt, p_sample, g_mix, w_in, conv_w, w_out,
           rel_bias, g_ffn, peer_wq, peer_k1, peer_k2, peer_u, peer_v, g_ple, w_ple_gate, w_ple_proj, g_final):
    B, T, D = x_prompt.shape
    S, TS, _ = x_sample.shape
    depth = w_in.shape[0]
    conv_ch = conv_w.shape[-1]
    attn_w = D - conv_ch
    n_heads = attn_w // HEAD_DIM
    L = cache_k.shape[2]
    keep = min(MAX_WINDOW, T)
    QB = 128
    assert all(w == BAND * d for w, d in DILATED_PATTERNS)
    assert T % (QB * max(d for _, d in DILATED_PATTERNS)) == 0 and attn_w % LANES == 0

    w_in_b = w_in.astype(BF16)
    w_out_b = w_out.astype(BF16)
    wq_b = peer_wq.astype(BF16)
    wg_b = w_ple_gate.astype(BF16)
    wp_b = w_ple_proj.astype(BF16)
    u_b, vt_b = _expert_prep(peer_u, peer_v)
    g_mix3, g_ffn3, g_ple3 = (g.reshape(depth, 1, D) for g in (g_mix, g_ffn, g_ple))
    g_fin2 = g_final.reshape(1, D)
    ck = cache_k.reshape(depth, S, L, attn_w)
    cv = cache_v.reshape(depth, S, L, attn_w)
    pp = p_prompt.reshape(depth, B * T, -1)
    ps = p_sample.reshape(depth, S * TS, -1)

    bias_rows = _prompt_bias_rows(rel_bias, QB + BAND)
    LP = L + LANES
    sbias, smult = _sample_tables(rel_bias, L, TS, LP)

    hp = x_prompt.reshape(B * T, D)
    hs = x_sample.reshape(S * TS, D)
    zero_past = jnp.zeros((B, 2, conv_ch), F32)
    kp_l, vp_l, cp_l, ks_l, vs_l, cs_l = [], [], [], [], [], []

    def mixer_tail(h, yc, ya, lam_src_layer):
        i = lam_src_layer
        h1, xt, pq = _outproj(h, yc, ya, w_out_b, g_ffn3, wq_b, i)
        lam, e1z, rho, e2 = _peer_topk(pq, peer_k1, peer_k2, i)
        h2 = _peer_main(xt, u_b, vt_b, lam, e1z, rho, e2, h1, i)
        return h2

    for i in range(depth):
        final = i == depth - 1
        u, bg, q, k, v = _inproj(hp, g_mix3, w_in_b, i, conv_ch, attn_w)
        yc, cst = _conv(u, bg, zero_past, conv_w, i, B, T)
        ya = _attn_prompt(q, k, v, bias_rows, B, T, QB)
        h2 = mixer_tail(hp, yc, ya, i)
        hp = _ple(h2, pp, g_ple3, wg_b, wp_b, g_fin2, i, final)
        kp_l.append(k.reshape(B, T, n_heads, HEAD_DIM)[:, T - keep:])
        vp_l.append(v.reshape(B, T, n_heads, HEAD_DIM)[:, T - keep:])
        cp_l.append(cst)
        u, bg, q, k, v = _inproj(hs, g_mix3, w_in_b, i, conv_ch, attn_w)
        yc, cst = _conv(u, bg, state_conv[i], conv_w, i, S, TS)
        ya = _attn_sample(q, k, v, ck, cv, sbias, smult, i, S, TS)
        h2 = mixer_tail(hs, yc, ya, i)
        hs = _ple(h2, ps, g_ple3, wg_b, wp_b, g_fin2, i, final)
        ks_l.append(k.reshape(S, TS, n_heads, HEAD_DIM))
        vs_l.append(v.reshape(S, TS, n_heads, HEAD_DIM))
        cs_l.append(cst)

    return (hp.reshape(B, T, D), hs.reshape(S, TS, D), jnp.stack(kp_l), jnp.stack(vp_l), jnp.stack(cp_l),
            jnp.stack(ks_l), jnp.stack(vs_l), jnp.stack(cs_l))
```

```python
import functools
import math

import numpy as np
import jax
import jax.numpy as jnp
from jax import lax
from jax.experimental import pallas as pl
from jax.experimental.pallas import tpu as pltpu

F32 = jnp.float32
BF16 = jnp.bfloat16

HEAD_DIM = 64
LANES = 128
DILATED_PATTERNS = ((128, 1), (512, 4), (2048, 16))
BAND = 128
MAX_WINDOW = 2048
N_BUCKETS = 32
MAX_DISTANCE = 2048
N_KEYS = 128
PEER_HEADS = 8
PEER_TOPK = 16
EPS = 1e-6
NEG = -1e30
VMEM_LIMIT_BYTES = 56 * 1024 * 1024


def _cparams(*sem):
    return pltpu.CompilerParams(dimension_semantics=sem, vmem_limit_bytes=VMEM_LIMIT_BYTES)


def _row_tile(rows, cap):
    t = min(rows, cap)
    while rows % t:
        t //= 2
    return t


def _rms(x, g):
    return x * lax.rsqrt(jnp.mean(x * x, axis=-1, keepdims=True) + EPS) * g


def _inproj_kernel(x_ref, g_ref, w_ref, u_ref, bg_ref, q_ref, k_ref, v_ref, *, conv_ch, attn_w):
    xn = _rms(x_ref[...], g_ref[...])
    proj = jnp.dot(xn.astype(BF16), w_ref[...], preferred_element_type=F32)
    c, a = conv_ch, attn_w
    u_ref[...] = proj[:, 2 * c:3 * c] * proj[:, :c]
    bg_ref[...] = proj[:, c:2 * c]
    q_ref[...] = proj[:, 3 * c:3 * c + a]
    k_ref[...] = proj[:, 3 * c + a:3 * c + 2 * a]
    v_ref[...] = proj[:, 3 * c + 2 * a:]


def _inproj(x, g_all, w_all, layer, conv_ch, attn_w):
    rows, d = x.shape
    tm = _row_tile(rows, 512)
    n_out = w_all.shape[-1]
    row = lambda w: pl.BlockSpec((tm, w), lambda i: (i, 0))
    return pl.pallas_call(
        functools.partial(_inproj_kernel, conv_ch=conv_ch, attn_w=attn_w),
        grid=(rows // tm,),
        in_specs=[row(d),
                  pl.BlockSpec((None, 1, d), lambda i: (layer, 0, 0)),
                  pl.BlockSpec((None, d, n_out), lambda i: (layer, 0, 0))],
        out_specs=[row(conv_ch), row(conv_ch), row(attn_w), row(attn_w), row(attn_w)],
        out_shape=[jax.ShapeDtypeStruct((rows, conv_ch), F32)] * 2
        + [jax.ShapeDtypeStruct((rows, attn_w), F32)] * 3,
        compiler_params=_cparams("parallel"),
        name="inproj",
    )(x, g_all, w_all)


def _conv_kernel(u_ref, bg_ref, past_ref, w_ref, y_ref, st_ref, *, T):
    u = u_ref[...]
    row = lax.broadcasted_iota(jnp.int32, u.shape, 0)
    p0 = past_ref[0:1, :]
    p1 = past_ref[1:2, :]
    um1 = jnp.where(row == 0, p1, pltpu.roll(u, 1, 0))
    um2 = jnp.where(row == 0, p0, jnp.where(row == 1, p1, pltpu.roll(u, 2, 0)))
    conv = w_ref[0:1, :] * um2 + w_ref[1:2, :] * um1 + w_ref[2:3, :] * u
    y_ref[...] = bg_ref[...] * conv
    st_ref[...] = u_ref[T - 2:T, :]


def _conv(u, bg, past, w_all, layer, nseq, T):
    rows, c = u.shape
    blk = pl.BlockSpec((T, c), lambda s: (s, 0))
    return pl.pallas_call(
        functools.partial(_conv_kernel, T=T),
        grid=(nseq,),
        in_specs=[blk, blk,
                  pl.BlockSpec((None, 2, c), lambda s: (s, 0, 0)),
                  pl.BlockSpec((None, 3, c), lambda s: (layer, 0, 0))],
        out_specs=[blk, pl.BlockSpec((None, 2, c), lambda s: (s, 0, 0))],
        out_shape=[jax.ShapeDtypeStruct((rows, c), F32),
                   jax.ShapeDtypeStruct((nseq, 2, c), F32)],
        compiler_params=_cparams("parallel"),
        name="conv",
    )(u, bg, past, w_all)


def _t5_bucket_np(dist):
    max_exact = N_BUCKETS // 2
    df = np.maximum(dist, 1).astype(np.float32)
    large = max_exact + (np.log(df / np.float32(max_exact)) / np.float32(math.log(MAX_DISTANCE / max_exact))
                         * np.float32(N_BUCKETS - max_exact)).astype(np.int32)
    large = np.minimum(large, N_BUCKETS - 1)
    return np.where(dist < max_exact, dist, large).astype(np.int32)


def _static_rows(table, idx):
    onehot = jnp.asarray(np.eye(table.shape[0], dtype=np.float32)[idx])
    return jnp.einsum('...k,kh->...h', onehot, table, precision=lax.Precision.HIGHEST)


def _prompt_bias_rows(rel_bias, kb):
    m = np.arange(kb)
    idx = np.full((len(DILATED_PATTERNS), kb), N_BUCKETS, np.int32)
    for r, (_, dil) in enumerate(DILATED_PATTERNS):
        idx[r] = np.where(m <= BAND, _t5_bucket_np(np.clip(BAND - m, 0, BAND) * dil), N_BUCKETS)
    ext = jnp.concatenate([rel_bias.astype(F32), jnp.full((1, rel_bias.shape[1]), NEG, F32)], axis=0)
    rows = jnp.transpose(_static_rows(ext, idx), (0, 2, 1))
    return rows.reshape(rows.shape[0], rows.shape[1] // 2, 2, kb)


def _sample_tables(rel_bias, L, T, lp):
    t = np.arange(T)[:, None]
    pos = np.arange(lp)[None, :]
    dist = L + t - pos
    valid = (dist >= 0) & (pos < L + T)
    mult = np.zeros((T, lp), np.float32)
    for window, dil in DILATED_PATTERNS:
        mult += (valid & (dist % dil == 0) & (dist <= window)).astype(np.float32)
    idx = _t5_bucket_np(np.clip(dist, 0, MAX_DISTANCE))
    bias = jnp.transpose(_static_rows(rel_bias.astype(F32), idx), (2, 0, 1))
    return bias, jnp.asarray(mult)


MERGE_ROWS = 256


def _attn_prompt_kernel(q_ref, k_ref, v_ref, brow_ref, o_ref, kp, vp, tab, m_sc, l_sc, acc_sc, *, T, QB, PAD):
    KB = QB + BAND
    npat = len(DILATED_PATTERNS)
    zeros = jnp.zeros((PAD, LANES), F32)
    kp[0:PAD, :] = zeros
    vp[0:PAD, :] = zeros
    kp[PAD:PAD + T, :] = k_ref[...]
    vp[PAD:PAD + T, :] = v_ref[...]
    before_start = lax.broadcasted_iota(jnp.int32, (QB, KB), 1) < BAND
    for r in range(npat):
        for hh in range(2):
            row = jnp.broadcast_to(brow_ref[r, hh:hh + 1, :], (QB, KB))
            toe = pltpu.roll(row, 0, 1, stride=1, stride_axis=0)
            tab[r, 0, hh * QB:(hh + 1) * QB, :] = toe
            tab[r, 1, hh * QB:(hh + 1) * QB, :] = jnp.where(before_start, NEG, toe)
    is0 = lax.broadcasted_iota(jnp.int32, (1, LANES), 1) < HEAD_DIM
    scale = HEAD_DIM ** -0.5

    for r, (_, d) in enumerate(DILATED_PATTERNS):
        nblk = (T // d) // QB

        def body(idx, carry, r=r, d=d, nblk=nblk):
            c = idx // nblk
            bi = idx % nblk
            a0 = bi * QB
            qstart = c + d * a0
            kstart = PAD + c + d * (a0 - BAND)
            if d == 1:
                qsl = pl.ds(pl.multiple_of(qstart, QB), QB)
                ksl = pl.ds(pl.multiple_of(kstart, BAND), KB)
            else:
                qsl = pl.ds(qstart, QB, stride=d)
                ksl = pl.ds(kstart, KB, stride=d)
            q = q_ref[qsl, :] * scale
            kb = kp[ksl, :].astype(BF16)
            vb = vp[ksl, :].astype(BF16)
            variant = jnp.where(bi == 0, 1, 0)
            q2 = jnp.concatenate([jnp.where(is0, q, 0.0), jnp.where(is0, 0.0, q)], axis=0).astype(BF16)
            sc = lax.dot_general(q2, kb, (((1,), (1,)), ((), ())), preferred_element_type=F32)
            sc = sc + tab[r, variant]
            m = jnp.max(sc, axis=1, keepdims=True)
            p = jnp.exp(sc - m)
            l = jnp.sum(p, axis=1, keepdims=True)
            pv = jnp.dot(p.astype(BF16), vb, preferred_element_type=F32)
            m_sc[r, qsl, :] = jnp.where(is0, m[:QB], m[QB:])
            l_sc[r, qsl, :] = jnp.where(is0, l[:QB], l[QB:])
            acc_sc[r, qsl, :] = jnp.where(is0, pv[:QB], pv[QB:])
            return carry

        lax.fori_loop(0, d * nblk, body, 0, unroll=16)

    def merge(i, carry):
        rows = pl.ds(pl.multiple_of(i * MERGE_ROWS, MERGE_ROWS), MERGE_ROWS)
        m = [m_sc[r, rows, :] for r in range(npat)]
        mx = functools.reduce(jnp.maximum, m)
        w = [jnp.exp(mr - mx) for mr in m]
        num = sum(w[r] * acc_sc[r, rows, :] for r in range(npat))
        den = sum(w[r] * l_sc[r, rows, :] for r in range(npat))
        o_ref[rows, :] = (num / den).astype(o_ref.dtype)
        return carry

    lax.fori_loop(0, T // MERGE_ROWS, merge, 0)


def _attn_prompt(q, k, v, bias_rows, B, T, QB):
    rows, aw = q.shape
    npair = aw // LANES
    npat = len(DILATED_PATTERNS)
    PAD = BAND * max(d for _, d in DILATED_PATTERNS)
    KB = QB + BAND
    blk = pl.BlockSpec((T, LANES), lambda b, hp: (b, hp))
    return pl.pallas_call(
        functools.partial(_attn_prompt_kernel, T=T, QB=QB, PAD=PAD),
        grid=(B, npair),
        in_specs=[blk, blk, blk,
                  pl.BlockSpec((npat, None, 2, KB), lambda b, hp: (0, hp, 0, 0))],
        out_specs=blk,
        out_shape=jax.ShapeDtypeStruct((rows, aw), BF16),
        scratch_shapes=[pltpu.VMEM((PAD + T, LANES), F32), pltpu.VMEM((PAD + T, LANES), F32),
                        pltpu.VMEM((npat, 2, 2 * QB, KB), F32),
                        pltpu.VMEM((npat, T, LANES), F32), pltpu.VMEM((npat, T, LANES), F32),
                        pltpu.VMEM((npat, T, LANES), F32)],
        compiler_params=_cparams("parallel", "parallel"),
        name="attn_prompt",
    )(q, k, v, bias_rows)


def _attn_sample_kernel(q_ref, k_ref, v_ref, ck_ref, cv_ref, bias_ref, mult_ref, o_ref, kf, vf, *, L, T, LP):
    is0 = lax.broadcasted_iota(jnp.int32, (1, LANES), 1) < HEAD_DIM
    mult = mult_ref[...]
    pad = jnp.zeros((LP - L, LANES), BF16)
    for hp in range(q_ref.shape[1] // LANES):
        lanes = slice(hp * LANES, (hp + 1) * LANES)
        kf[0:L, :] = ck_ref[:, lanes]
        vf[0:L, :] = cv_ref[:, lanes]
        kf[L:LP, :] = pad
        vf[L:LP, :] = pad
        kf[L:L + T, :] = k_ref[:, lanes].astype(BF16)
        vf[L:L + T, :] = v_ref[:, lanes].astype(BF16)
        q = q_ref[:, lanes] * (HEAD_DIM ** -0.5)
        kb = kf[...]
        vb = vf[...]
        outs = []
        for hh in range(2):
            qh = jnp.where(is0 if hh == 0 else jnp.logical_not(is0), q, 0.0).astype(BF16)
            s = lax.dot_general(qh, kb, (((1,), (1,)), ((), ())), preferred_element_type=F32)
            s = jnp.where(mult > 0.0, s + bias_ref[2 * hp + hh], NEG)
            m = jnp.max(s, axis=1, keepdims=True)
            p = jnp.exp(s - m) * mult
            l = jnp.sum(p, axis=1, keepdims=True)
            outs.append(jnp.dot(p.astype(BF16), vb, preferred_element_type=F32) / l)
        o_ref[:, lanes] = jnp.where(is0, outs[0], outs[1]).astype(o_ref.dtype)


def _attn_sample(q, k, v, cache_k, cache_v, bias, mult, layer, S, T):
    rows, aw = q.shape
    L = cache_k.shape[2]
    LP = bias.shape[-1]
    new = pl.BlockSpec((T, aw), lambda s: (s, 0))
    cache = pl.BlockSpec((None, None, L, aw), lambda s: (layer, s, 0, 0))
    return pl.pallas_call(
        functools.partial(_attn_sample_kernel, L=L, T=T, LP=LP),
        grid=(S,),
        in_specs=[new, new, new, cache, cache,
                  pl.BlockSpec(bias.shape, lambda s: (0, 0, 0)),
                  pl.BlockSpec((T, LP), lambda s: (0, 0))],
        out_specs=new,
        out_shape=jax.ShapeDtypeStruct((rows, aw), BF16),
        scratch_shapes=[pltpu.VMEM((LP, LANES), BF16), pltpu.VMEM((LP, LANES), BF16)],
        compiler_params=_cparams("parallel"),
        name="attn_sample",
    )(q, k, v, cache_k, cache_v, bias, mult)


def _outproj_kernel(h_ref, yc_ref, ya_ref, wo_ref, g_ref, wq_ref, h1_ref, xt_ref, pq_ref, *, conv_ch):
    mix = jnp.dot(yc_ref[...].astype(BF16), wo_ref[0:conv_ch, :], preferred_element_type=F32)
    mix = mix + jnp.dot(ya_ref[...], wo_ref[conv_ch:, :], preferred_element_type=F32)
    h1 = h_ref[...] + mix
    h1_ref[...] = h1
    xn = _rms(h1, g_ref[...])
    pq = jnp.dot(xn.astype(BF16), wq_ref[...], preferred_element_type=F32)
    hw = 2 * N_KEYS
    for hd in range(PEER_HEADS):
        pq_ref[hd] = pq[:, hd * hw:(hd + 1) * hw]
    xt_ref[...] = pltpu.bitcast(xn.T.astype(BF16), jnp.uint32)


def _outproj(h, yc, ya, wo_all, g_all, wq_all, layer):
    rows, d = h.shape
    conv_ch = yc.shape[1]
    nq = wq_all.shape[-1]
    tm = _row_tile(rows, 512)
    row = lambda w: pl.BlockSpec((tm, w), lambda i: (i, 0))
    return pl.pallas_call(
        functools.partial(_outproj_kernel, conv_ch=conv_ch),
        grid=(rows // tm,),
        in_specs=[row(d), row(conv_ch), row(ya.shape[1]),
                  pl.BlockSpec((None, d, d), lambda i: (layer, 0, 0)),
                  pl.BlockSpec((None, 1, d), lambda i: (layer, 0, 0)),
                  pl.BlockSpec((None, d, nq), lambda i: (layer, 0, 0))],
        out_specs=[row(d), pl.BlockSpec((d // 2, tm), lambda i: (0, i)),
                   pl.BlockSpec((PEER_HEADS, tm, nq // PEER_HEADS), lambda i: (0, i, 0))],
        out_shape=[jax.ShapeDtypeStruct((rows, d), F32),
                   jax.ShapeDtypeStruct((d // 2, rows), jnp.uint32),
                   jax.ShapeDtypeStruct((PEER_HEADS, rows, nq // PEER_HEADS), F32)],
        compiler_params=_cparams("parallel"),
        name="outproj",
    )(h, yc, ya, wo_all, g_all, wq_all)


def _top16_exact(x, pos):
    big = float(PEER_TOPK * PEER_TOPK)
    rank = jnp.full(x.shape, float(PEER_TOPK), F32)
    vals = []
    for r in range(PEER_TOPK):
        m = jnp.max(x, axis=0, keepdims=True)
        idx = jnp.min(jnp.where(x == m, pos, big), axis=0, keepdims=True)
        hit = pos == idx
        x = jnp.where(hit, -jnp.inf, x)
        rank = jnp.where(hit, float(r), rank)
        vals.append(m)
    return vals, rank


def _top16_distinct(x):
    rank = jnp.full(x.shape, float(PEER_TOPK), F32)
    vals = []
    for r in range(PEER_TOPK):
        m = jnp.max(x, axis=0, keepdims=True)
        hit = x == m
        x = jnp.where(hit, -jnp.inf, x)
        rank = jnp.where(hit, float(r), rank)
        vals.append(m)
    count = jnp.sum(jnp.where(rank < float(PEER_TOPK), 1.0, 0.0), axis=0, keepdims=True)
    return vals, rank, count


def _stack_rows(rows):
    n = len(rows)
    iota = lax.broadcasted_iota(jnp.int32, (n, rows[0].shape[1]), 0)
    out = jnp.zeros(iota.shape, F32)
    for r, v in enumerate(rows):
        out = jnp.where(iota == r, v, out)
    return out


_CAND_BLOCKS = ((0, 16), (1, 16), (2, 8), (3, 8), (4, 8), (5, 8), (6, 8), (7, 8))
_CAND_TAIL_R1 = 8


def _cand_pos():
    pos = []
    for r1, n2 in _CAND_BLOCKS:
        pos += [r1 * PEER_TOPK + r2 for r2 in range(n2)]
    pos += [r1 * PEER_TOPK for r1 in range(_CAND_TAIL_R1, PEER_TOPK)]
    return np.asarray(pos, np.float32)[:, None]


def _cand_build(a_rows, a_mat, b_mat, b_row0, op):
    parts = [op(a_rows[r1], b_mat[0:n2, :]) for r1, n2 in _CAND_BLOCKS]
    parts.append(op(a_mat[_CAND_TAIL_R1:, :], b_row0))
    return jnp.concatenate(parts, axis=0)


def _cand_counts(member):
    counts, off = [], 0
    for _, n2 in _CAND_BLOCKS:
        counts.append(jnp.sum(member[off:off + n2, :], axis=0, keepdims=True))
        off += n2
    for k in range(PEER_TOPK - _CAND_TAIL_R1):
        counts.append(member[off + k:off + k + 1, :])
    return counts


def _dup_bf16_words(x):
    b = pltpu.bitcast(x.astype(BF16).astype(F32), jnp.uint32)
    return b | (b >> 16)


def _peer_head_factors(s1, s2, pos_c, exact):
    if exact:
        key_pos = lax.broadcasted_iota(jnp.int32, s1.shape, 0).astype(F32)
        t1, rank1 = _top16_exact(s1, key_pos)
        t2, rank2 = _top16_exact(s2, key_pos)
        ok = None
    else:
        t1, rank1, n1 = _top16_distinct(s1)
        t2, rank2, n2 = _top16_distinct(s2)
    t1m, t2m = _stack_rows(t1), _stack_rows(t2)
    e1r = jnp.exp(t1m - t1[0])
    e2r = jnp.exp(t2m - t2[0])
    cand = _cand_build(t1, t1m, t2m, t2[0], lambda a, b: a + b)
    prod = _cand_build([e1r[r:r + 1, :] for r in range(PEER_TOPK)], e1r, e2r, e2r[0:1, :], lambda a, b: a * b)
    if exact:
        _, crank = _top16_exact(cand, pos_c)
        sel = crank < float(PEER_TOPK)
    else:
        x = cand
        for _ in range(PEER_TOPK):
            x = jnp.where(x == jnp.max(x, axis=0, keepdims=True), -jnp.inf, x)
        sel = x == -jnp.inf
    member = jnp.where(sel, 1.0, 0.0)
    if not exact:
        nc = jnp.sum(member, axis=0, keepdims=True)
        full = float(PEER_TOPK)
        ok = jnp.logical_and(jnp.logical_and(n1 == full, n2 == full), nc == full)
    z = jnp.sum(member * prod, axis=0, keepdims=True)
    counts = _cand_counts(member)
    lam = jnp.zeros(s1.shape, F32)
    for r in range(PEER_TOPK):
        lam = jnp.where(rank1 == float(r), counts[r], lam)
    e1z = jnp.exp(s1 - t1[0]) * (1.0 / z)
    e2 = jnp.exp(s2 - t2[0])
    return (lam, e1z, rank2, e2), ok


def _peer_topk_kernel(pq_ref, k1_ref, k2_ref, pos_ref, lam_ref, e1z_ref, rho_ref, e2_ref):
    k1 = k1_ref[...].astype(BF16)
    k2 = k2_ref[...].astype(BF16)
    pos_c = pos_ref[...]
    nt = (((1,), (1,)), ((), ()))

    def head(h, bad, exact):
        q = pq_ref[h]
        s1 = lax.dot_general(k1, q[:, :N_KEYS].astype(BF16), nt, preferred_element_type=F32)
        s2 = lax.dot_general(k2, q[:, N_KEYS:].astype(BF16), nt, preferred_element_type=F32)
        (lam, e1z, rho, e2), ok = _peer_head_factors(s1, s2, pos_c, exact)
        lam_ref[h] = _dup_bf16_words(lam)
        e1z_ref[h] = _dup_bf16_words(e1z)
        rho_ref[h] = pltpu.bitcast(rho.astype(BF16), jnp.uint32)
        e2_ref[h] = pltpu.bitcast(e2.astype(BF16), jnp.uint32)
        return bad if exact else jnp.maximum(bad, jnp.where(ok, 0.0, 1.0))

    bad = jnp.zeros((1, pq_ref.shape[1]), F32)
    for h in range(PEER_HEADS):
        bad = head(h, bad, exact=False)

    @pl.when(jnp.max(bad) > 0.0)
    def _():
        for h in range(PEER_HEADS):
            head(h, bad, exact=True)


def _peer_topk(pq, k1_all, k2_all, layer):
    _, rows, hw = pq.shape
    tt = LANES
    pos_c = jnp.asarray(np.broadcast_to(_cand_pos(), (_cand_pos().shape[0], tt)))
    blk = pl.BlockSpec((PEER_HEADS, N_KEYS, tt), lambda i: (0, 0, i))
    hblk = pl.BlockSpec((PEER_HEADS, N_KEYS // 2, tt), lambda i: (0, 0, i))
    kblk = pl.BlockSpec((None, N_KEYS, N_KEYS), lambda i: (layer, 0, 0))
    word = jax.ShapeDtypeStruct((PEER_HEADS, N_KEYS, rows), jnp.uint32)
    half = jax.ShapeDtypeStruct((PEER_HEADS, N_KEYS // 2, rows), jnp.uint32)
    return pl.pallas_call(
        _peer_topk_kernel,
        grid=(rows // tt,),
        in_specs=[pl.BlockSpec((PEER_HEADS, tt, hw), lambda i: (0, i, 0)), kblk, kblk,
                  pl.BlockSpec(pos_c.shape, lambda i: (0, 0))],
        out_specs=[blk, blk, hblk, hblk],
        out_shape=[word, word, half, half],
        compiler_params=_cparams("parallel"),
        name="peer_topk",
    )(pq, k1_all, k2_all, pos_c)


def _gelu(x):
    return 0.5 * x * (1.0 + lax.erf(x * (2.0 ** -0.5)))


def _pack_rows(x):
    return pltpu.bitcast(x.astype(BF16), jnp.uint32)


def _unpack_rows(w):
    return pltpu.bitcast(w, BF16)


def _vt_prep_kernel(v_ref, o_ref):
    o_ref[...] = _pack_rows(v_ref[...].T)


def _u_prep_kernel(u_ref, o_ref):
    o_ref[...] = _pack_rows(u_ref[...])


def _expert_prep(peer_u, peer_v):
    depth, n_exp, d = peer_u.shape
    eb = 1024
    src = pl.BlockSpec((None, eb, d), lambda l, e: (l, e, 0))
    u_w = pl.pallas_call(
        _u_prep_kernel,
        grid=(depth, n_exp // eb),
        in_specs=[src],
        out_specs=pl.BlockSpec((None, eb // 2, d), lambda l, e: (l, e, 0)),
        out_shape=jax.ShapeDtypeStruct((depth, n_exp // 2, d), jnp.uint32),
        compiler_params=_cparams("parallel", "parallel"),
        name="u_prep",
    )(peer_u)
    vt_w = pl.pallas_call(
        _vt_prep_kernel,
        grid=(depth, n_exp // eb),
        in_specs=[src],
        out_specs=pl.BlockSpec((None, d // 2, eb), lambda l, e: (l, 0, e)),
        out_shape=jax.ShapeDtypeStruct((depth, d // 2, n_exp), jnp.uint32),
        compiler_params=_cparams("parallel", "parallel"),
        name="vt_prep",
    )(peer_v)
    return u_w, vt_w


PACKED_ROWS = 16


def _peer_main_kernel(xt_ref, u_ref, vt_ref, lam_ref, e1z_ref, rho_ref, e2_ref, h1_ref, o_ref,
                      acc, c_sc, *, TT, EB):
    e = pl.program_id(1)
    half = PACKED_ROWS // 2

    @pl.when(e == 0)
    def _():
        acc[...] = jnp.zeros(acc.shape, F32)

    a = jnp.dot(_unpack_rows(u_ref[...]), _unpack_rows(xt_ref[...]), preferred_element_type=F32)
    for c in range(EB // N_KEYS):
        for tc in range(TT // LANES):
            cols = slice(tc * LANES, (tc + 1) * LANES)
            bcast = lambda ref, h: pltpu.bitcast(jnp.broadcast_to(ref[h, c:c + 1, cols], (half, LANES)), BF16)
            lam = [bcast(lam_ref, h) for h in range(PEER_HEADS)]
            e1z = [bcast(e1z_ref, h) for h in range(PEER_HEADS)]
            for jv in range(N_KEYS // PACKED_ROWS):
                wrows = slice(jv * half, (jv + 1) * half)
                w = jnp.zeros((PACKED_ROWS, LANES), BF16)
                for h in range(PEER_HEADS):
                    rho = pltpu.bitcast(rho_ref[h, wrows, cols], BF16)
                    e2 = pltpu.bitcast(e2_ref[h, wrows, cols], BF16)
                    w = w + jnp.where(rho < lam[h], e2 * e1z[h], 0.0)
                e0 = c * N_KEYS + jv * PACKED_ROWS
                act = _gelu(a[e0:e0 + PACKED_ROWS, cols]).astype(BF16)
                c_sc[e0 // 2:e0 // 2 + half, cols] = pltpu.bitcast(w * act, jnp.uint32)
    acc[...] += jnp.dot(_unpack_rows(vt_ref[...]), _unpack_rows(c_sc[...]), preferred_element_type=F32)

    @pl.when(e == pl.num_programs(1) - 1)
    def _():
        o_ref[...] = h1_ref[...] + acc[...].T


def _peer_main(xt, u_all, vt_all, lam, e1z, rho, e2, h1, layer):
    rows, d = h1.shape
    n_exp = vt_all.shape[2]
    tt = _row_tile(rows, 512)
    eb = 1024
    cb = eb // N_KEYS
    fac_c = pl.BlockSpec((PEER_HEADS, cb, tt), lambda i, e: (0, e, i))
    fac_all = pl.BlockSpec((PEER_HEADS, N_KEYS // 2, tt), lambda i, e: (0, 0, i))
    return pl.pallas_call(
        functools.partial(_peer_main_kernel, TT=tt, EB=eb),
        grid=(rows // tt, n_exp // eb),
        in_specs=[pl.BlockSpec((d // 2, tt), lambda i, e: (0, i)),
                  pl.BlockSpec((None, eb // 2, d), lambda i, e: (layer, e, 0)),
                  pl.BlockSpec((None, d // 2, eb), lambda i, e: (layer, 0, e)),
                  fac_c, fac_c, fac_all, fac_all,
                  pl.BlockSpec((tt, d), lambda i, e: (i, 0))],
        out_specs=pl.BlockSpec((tt, d), lambda i, e: (i, 0)),
        out_shape=jax.ShapeDtypeStruct((rows, d), F32),
        scratch_shapes=[pltpu.VMEM((d, tt), F32), pltpu.VMEM((eb // 2, tt), jnp.uint32)],
        compiler_params=_cparams("parallel", "arbitrary"),
        name="peer_main",
    )(xt, u_all, vt_all, lam, e1z, rho, e2, h1)


def _ple_kernel(h_ref, p_ref, g_ref, wg_ref, wp_ref, gf_ref, o_ref, *, final):
    h = h_ref[...]
    gate = jax.nn.sigmoid(jnp.dot(_rms(h, g_ref[...]).astype(BF16), wg_ref[...], preferred_element_type=F32))
    h = h + jnp.dot(p_ref[...].astype(BF16), wp_ref[...], preferred_element_type=F32) * gate
    o_ref[...] = _rms(h, gf_ref[...]) if final else h


def _ple(h, p_all, g_all, wg_all, wp_all, g_final, layer, final):
    rows, d = h.shape
    pd = p_all.shape[-1]
    tm = _row_tile(rows, 512)
    return pl.pallas_call(
        functools.partial(_ple_kernel, final=final),
        grid=(rows // tm,),
        in_specs=[pl.BlockSpec((tm, d), lambda i: (i, 0)),
                  pl.BlockSpec((None, tm, pd), lambda i: (layer, i, 0)),
                  pl.BlockSpec((None, 1, d), lambda i: (layer, 0, 0)),
                  pl.BlockSpec((None, d, d), lambda i: (layer, 0, 0)),
                  pl.BlockSpec((None, pd, d), lambda i: (layer, 0, 0)),
                  pl.BlockSpec((1, d), lambda i: (0, 0))],
        out_specs=pl.BlockSpec((tm, d), lambda i: (i, 0)),
        out_shape=jax.ShapeDtypeStruct((rows, d), F32),
        compiler_params=_cparams("parallel"),
        name="ple",
    )(h, p_all, g_all, wg_all, wp_all, g_final)


def kernel(x_prompt, x_sample, cache_k, cache_v, state_conv, p_prompt, p_sample, g_mix, w_in, conv_w, w_out,
           rel_bias, g_ffn, peer_wq, peer_k1, peer_k2, peer_u, peer_v, g_ple, w_ple_gate, w_ple_proj, g_final):
    B, T, D = x_prompt.shape
    S, TS, _ = x_sample.shape
    depth = w_in.shape[0]
    conv_ch = conv_w.shape[-1]
    attn_w = D - conv_ch
    n_heads = attn_w // HEAD_DIM
    L = cache_k.shape[2]
    keep = min(MAX_WINDOW, T)
    QB = 128
    assert all(w == BAND * d for w, d in DILATED_PATTERNS)
    assert T % (QB * max(d for _, d in DILATED_PATTERNS)) == 0 and attn_w % LANES == 0

    w_in_b = w_in.astype(BF16)
    w_out_b = w_out.astype(BF16)
    wq_b = peer_wq.astype(BF16)
    wg_b = w_ple_gate.astype(BF16)
    wp_b = w_ple_proj.astype(BF16)
    u_b, vt_b = _expert_prep(peer_u, peer_v)
    g_mix3, g_ffn3, g_ple3 = (g.reshape(depth, 1, D) for g in (g_mix, g_ffn, g_ple))
    g_fin2 = g_final.reshape(1, D)
    ck = cache_k.reshape(depth, S, L, attn_w).astype(BF16)
    cv = cache_v.reshape(depth, S, L, attn_w).astype(BF16)
    pp = p_prompt.reshape(depth, B * T, -1)
    ps = p_sample.reshape(depth, S * TS, -1)

    bias_rows = _prompt_bias_rows(rel_bias, QB + BAND)
    LP = L + LANES
    sbias, smult = _sample_tables(rel_bias, L, TS, LP)

    hp = x_prompt.reshape(B * T, D)
    hs = x_sample.reshape(S * TS, D)
    zero_past = jnp.zeros((B, 2, conv_ch), F32)
    kp_l, vp_l, cp_l, ks_l, vs_l, cs_l = [], [], [], [], [], []

    def mixer_tail(h, yc, ya, lam_src_layer):
        i = lam_src_layer
        h1, xt, pq = _outproj(h, yc, ya, w_out_b, g_ffn3, wq_b, i)
        lam, e1z, rho, e2 = _peer_topk(pq, peer_k1, peer_k2, i)
        h2 = _peer_main(xt, u_b, vt_b, lam, e1z, rho, e2, h1, i)
        return h2

    for i in range(depth):
        final = i == depth - 1
        u, bg, q, k, v = _inproj(hp, g_mix3, w_in_b, i, conv_ch, attn_w)
        yc, cst = _conv(u, bg, zero_past, conv_w, i, B, T)
        ya = _attn_prompt(q, k, v, bias_rows, B, T, QB)
        h2 = mixer_tail(hp, yc, ya, i)
        hp = _ple(h2, pp, g_ple3, wg_b, wp_b, g_fin2, i, final)
        kp_l.append(k.reshape(B, T, n_heads, HEAD_DIM)[:, T - keep:])
        vp_l.append(v.reshape(B, T, n_heads, HEAD_DIM)[:, T - keep:])
        cp_l.append(cst)
        u, bg, q, k, v = _inproj(hs, g_mix3, w_in_b, i, conv_ch, attn_w)
        yc, cst = _conv(u, bg, state_conv[i], conv_w, i, S, TS)
        ya = _attn_sample(q, k, v, ck, cv, sbias, smult, i, S, TS)
        h2 = mixer_tail(hs, yc, ya, i)
        hs = _ple(h2, ps, g_ple3, wg_b, wp_b, g_fin2, i, final)
        ks_l.append(k.reshape(S, TS, n_heads, HEAD_DIM))
        vs_l.append(v.reshape(S, TS, n_heads, HEAD_DIM))
        cs_l.append(cst)

    return (hp.reshape(B, T, D), hs.reshape(S, TS, D), jnp.stack(kp_l), jnp.stack(vp_l), jnp.stack(cp_l),
            jnp.stack(ks_l), jnp.stack(vs_l), jnp.stack(cs_l))
```

```python
import functools
import math

import numpy as np
import jax
import jax.numpy as jnp
from jax import lax
from jax.experimental import pallas as pl
from jax.experimental.pallas import tpu as pltpu

F32 = jnp.float32
BF16 = jnp.bfloat16

HEAD_DIM = 64
LANES = 128
DILATED_PATTERNS = ((128, 1), (512, 4), (2048, 16))
BAND = 128
MAX_WINDOW = 2048
N_BUCKETS = 32
MAX_DISTANCE = 2048
N_KEYS = 128
PEER_HEADS = 8
PEER_TOPK = 16
EPS = 1e-6
NEG = -1e30
VMEM_LIMIT_BYTES = 56 * 1024 * 1024


def _cparams(*sem):
    return pltpu.CompilerParams(dimension_semantics=sem, vmem_limit_bytes=VMEM_LIMIT_BYTES)


def _row_tile(rows, cap):
    t = min(rows, cap)
    while rows % t:
        t //= 2
    return t


def _rms(x, g):
    return x * lax.rsqrt(jnp.mean(x * x, axis=-1, keepdims=True) + EPS) * g


def _inproj_kernel(x_ref, g_ref, w_ref, u_ref, bg_ref, q_ref, k_ref, v_ref, *, conv_ch, attn_w):
    xn = _rms(x_ref[...], g_ref[...])
    proj = jnp.dot(xn.astype(BF16), w_ref[...], preferred_element_type=F32)
    c, a = conv_ch, attn_w
    u_ref[...] = proj[:, 2 * c:3 * c] * proj[:, :c]
    bg_ref[...] = proj[:, c:2 * c]
    q_ref[...] = proj[:, 3 * c:3 * c + a]
    k_ref[...] = proj[:, 3 * c + a:3 * c + 2 * a]
    v_ref[...] = proj[:, 3 * c + 2 * a:]


def _inproj(x, g_all, w_all, layer, conv_ch, attn_w):
    rows, d = x.shape
    tm = _row_tile(rows, 512)
    n_out = w_all.shape[-1]
    row = lambda w: pl.BlockSpec((tm, w), lambda i: (i, 0))
    return pl.pallas_call(
        functools.partial(_inproj_kernel, conv_ch=conv_ch, attn_w=attn_w),
        grid=(rows // tm,),
        in_specs=[row(d),
                  pl.BlockSpec((None, 1, d), lambda i: (layer, 0, 0)),
                  pl.BlockSpec((None, d, n_out), lambda i: (layer, 0, 0))],
        out_specs=[row(conv_ch), row(conv_ch), row(attn_w), row(attn_w), row(attn_w)],
        out_shape=[jax.ShapeDtypeStruct((rows, conv_ch), F32)] * 2
        + [jax.ShapeDtypeStruct((rows, attn_w), F32)] * 3,
        compiler_params=_cparams("parallel"),
        name="inproj",
    )(x, g_all, w_all)


def _conv_kernel(u_ref, bg_ref, past_ref, w_ref, y_ref, st_ref, *, T):
    u = u_ref[...]
    row = lax.broadcasted_iota(jnp.int32, u.shape, 0)
    p0 = past_ref[0:1, :]
    p1 = past_ref[1:2, :]
    um1 = jnp.where(row == 0, p1, pltpu.roll(u, 1, 0))
    um2 = jnp.where(row == 0, p0, jnp.where(row == 1, p1, pltpu.roll(u, 2, 0)))
    conv = w_ref[0:1, :] * um2 + w_ref[1:2, :] * um1 + w_ref[2:3, :] * u
    y_ref[...] = bg_ref[...] * conv
    st_ref[...] = u_ref[T - 2:T, :]


def _conv(u, bg, past, w_all, layer, nseq, T):
    rows, c = u.shape
    blk = pl.BlockSpec((T, c), lambda s: (s, 0))
    return pl.pallas_call(
        functools.partial(_conv_kernel, T=T),
        grid=(nseq,),
        in_specs=[blk, blk,
                  pl.BlockSpec((None, 2, c), lambda s: (s, 0, 0)),
                  pl.BlockSpec((None, 3, c), lambda s: (layer, 0, 0))],
        out_specs=[blk, pl.BlockSpec((None, 2, c), lambda s: (s, 0, 0))],
        out_shape=[jax.ShapeDtypeStruct((rows, c), F32),
                   jax.ShapeDtypeStruct((nseq, 2, c), F32)],
        compiler_params=_cparams("parallel"),
        name="conv",
    )(u, bg, past, w_all)


def _t5_bucket_np(dist):
    max_exact = N_BUCKETS // 2
    df = np.maximum(dist, 1).astype(np.float32)
    large = max_exact + (np.log(df / np.float32(max_exact)) / np.float32(math.log(MAX_DISTANCE / max_exact))
                         * np.float32(N_BUCKETS - max_exact)).astype(np.int32)
    large = np.minimum(large, N_BUCKETS - 1)
    return np.where(dist < max_exact, dist, large).astype(np.int32)


def _static_rows(table, idx):
    onehot = jnp.asarray(np.eye(table.shape[0], dtype=np.float32)[idx])
    return jnp.einsum('...k,kh->...h', onehot, table, precision=lax.Precision.HIGHEST)


def _prompt_bias_rows(rel_bias, kb):
    m = np.arange(kb)
    idx = np.full((len(DILATED_PATTERNS), kb), N_BUCKETS, np.int32)
    for r, (_, dil) in enumerate(DILATED_PATTERNS):
        idx[r] = np.where(m <= BAND, _t5_bucket_np(np.clip(BAND - m, 0, BAND) * dil), N_BUCKETS)
    ext = jnp.concatenate([rel_bias.astype(F32), jnp.full((1, rel_bias.shape[1]), NEG, F32)], axis=0)
    rows = jnp.transpose(_static_rows(ext, idx), (0, 2, 1))
    return rows.reshape(rows.shape[0], rows.shape[1] // 2, 2, kb)


def _sample_tables(rel_bias, L, T, lp):
    t = np.arange(T)[:, None]
    pos = np.arange(lp)[None, :]
    dist = L + t - pos
    valid = (dist >= 0) & (pos < L + T)
    mult = np.zeros((T, lp), np.float32)
    for window, dil in DILATED_PATTERNS:
        mult += (valid & (dist % dil == 0) & (dist <= window)).astype(np.float32)
    idx = _t5_bucket_np(np.clip(dist, 0, MAX_DISTANCE))
    bias = jnp.transpose(_static_rows(rel_bias.astype(F32), idx), (2, 0, 1))
    return bias, jnp.asarray(mult)


MERGE_ROWS = 256


def _attn_prompt_kernel(q_ref, k_ref, v_ref, brow_ref, o_ref, kp, vp, tab, m_sc, l_sc, acc_sc, *, T, QB, PAD):
    KB = QB + BAND
    npat = len(DILATED_PATTERNS)
    zeros = jnp.zeros((PAD, LANES), F32)
    kp[0:PAD, :] = zeros
    vp[0:PAD, :] = zeros
    kp[PAD:PAD + T, :] = k_ref[...]
    vp[PAD:PAD + T, :] = v_ref[...]
    before_start = lax.broadcasted_iota(jnp.int32, (QB, KB), 1) < BAND
    for r in range(npat):
        for hh in range(2):
            row = jnp.broadcast_to(brow_ref[r, hh:hh + 1, :], (QB, KB))
            toe = pltpu.roll(row, 0, 1, stride=1, stride_axis=0)
            tab[r, 0, hh * QB:(hh + 1) * QB, :] = toe
            tab[r, 1, hh * QB:(hh + 1) * QB, :] = jnp.where(before_start, NEG, toe)
    is0 = lax.broadcasted_iota(jnp.int32, (1, LANES), 1) < HEAD_DIM
    scale = HEAD_DIM ** -0.5

    for r, (_, d) in enumerate(DILATED_PATTERNS):
        nblk = (T // d) // QB

        def body(idx, carry, r=r, d=d, nblk=nblk):
            c = idx // nblk
            bi = idx % nblk
            a0 = bi * QB
            qstart = c + d * a0
            kstart = PAD + c + d * (a0 - BAND)
            if d == 1:
                qsl = pl.ds(pl.multiple_of(qstart, QB), QB)
                ksl = pl.ds(pl.multiple_of(kstart, BAND), KB)
            else:
                qsl = pl.ds(qstart, QB, stride=d)
                ksl = pl.ds(kstart, KB, stride=d)
            q = q_ref[qsl, :] * scale
            kb = kp[ksl, :].astype(BF16)
            vb = vp[ksl, :].astype(BF16)
            variant = jnp.where(bi == 0, 1, 0)
            q2 = jnp.concatenate([jnp.where(is0, q, 0.0), jnp.where(is0, 0.0, q)], axis=0).astype(BF16)
            sc = lax.dot_general(q2, kb, (((1,), (1,)), ((), ())), preferred_element_type=F32)
            sc = sc + tab[r, variant]
            m = jnp.max(sc, axis=1, keepdims=True)
            p = jnp.exp(sc - m)
            l = jnp.sum(p, axis=1, keepdims=True)
            pv = jnp.dot(p.astype(BF16), vb, preferred_element_type=F32)
            m_sc[r, qsl, :] = jnp.where(is0, m[:QB], m[QB:])
            l_sc[r, qsl, :] = jnp.where(is0, l[:QB], l[QB:])
            acc_sc[r, qsl, :] = jnp.where(is0, pv[:QB], pv[QB:])
            return carry

        lax.fori_loop(0, d * nblk, body, 0, unroll=True)

    def merge(i, carry):
        rows = pl.ds(pl.multiple_of(i * MERGE_ROWS, MERGE_ROWS), MERGE_ROWS)
        m = [m_sc[r, rows, :] for r in range(npat)]
        mx = functools.reduce(jnp.maximum, m)
        w = [jnp.exp(mr - mx) for mr in m]
        num = sum(w[r] * acc_sc[r, rows, :] for r in range(npat))
        den = sum(w[r] * l_sc[r, rows, :] for r in range(npat))
        o_ref[rows, :] = (num / den).astype(o_ref.dtype)
        return carry

    lax.fori_loop(0, T // MERGE_ROWS, merge, 0)


def _attn_prompt(q, k, v, bias_rows, B, T, QB):
    rows, aw = q.shape
    npair = aw // LANES
    npat = len(DILATED_PATTERNS)
    PAD = BAND * max(d for _, d in DILATED_PATTERNS)
    KB = QB + BAND
    blk = pl.BlockSpec((T, LANES), lambda b, hp: (b, hp))
    return pl.pallas_call(
        functools.partial(_attn_prompt_kernel, T=T, QB=QB, PAD=PAD),
        grid=(B, npair),
        in_specs=[blk, blk, blk,
                  pl.BlockSpec((npat, None, 2, KB), lambda b, hp: (0, hp, 0, 0))],
        out_specs=blk,
        out_shape=jax.ShapeDtypeStruct((rows, aw), BF16),
        scratch_shapes=[pltpu.VMEM((PAD + T, LANES), F32), pltpu.VMEM((PAD + T, LANES), F32),
                        pltpu.VMEM((npat, 2, 2 * QB, KB), F32),
                        pltpu.VMEM((npat, T, LANES), F32), pltpu.VMEM((npat, T, LANES), F32),
                        pltpu.VMEM((npat, T, LANES), F32)],
        compiler_params=_cparams("parallel", "parallel"),
        name="attn_prompt",
    )(q, k, v, bias_rows)


def _attn_sample_kernel(q_ref, k_ref, v_ref, ck_ref, cv_ref, bias_ref, mult_ref, o_ref, kf, vf, *, L, T, LP):
    is0 = lax.broadcasted_iota(jnp.int32, (1, LANES), 1) < HEAD_DIM
    mult = mult_ref[...]
    pad = jnp.zeros((LP - L, LANES), BF16)
    for hp in range(q_ref.shape[1] // LANES):
        lanes = slice(hp * LANES, (hp + 1) * LANES)
        kf[0:L, :] = ck_ref[:, lanes]
        vf[0:L, :] = cv_ref[:, lanes]
        kf[L:LP, :] = pad
        vf[L:LP, :] = pad
        kf[L:L + T, :] = k_ref[:, lanes].astype(BF16)
        vf[L:L + T, :] = v_ref[:, lanes].astype(BF16)
        q = q_ref[:, lanes] * (HEAD_DIM ** -0.5)
        kb = kf[...]
        vb = vf[...]
        outs = []
        for hh in range(2):
            qh = jnp.where(is0 if hh == 0 else jnp.logical_not(is0), q, 0.0).astype(BF16)
            s = lax.dot_general(qh, kb, (((1,), (1,)), ((), ())), preferred_element_type=F32)
            s = jnp.where(mult > 0.0, s + bias_ref[2 * hp + hh], NEG)
            m = jnp.max(s, axis=1, keepdims=True)
            p = jnp.exp(s - m) * mult
            l = jnp.sum(p, axis=1, keepdims=True)
            outs.append(jnp.dot(p.astype(BF16), vb, preferred_element_type=F32) / l)
        o_ref[:, lanes] = jnp.where(is0, outs[0], outs[1]).astype(o_ref.dtype)


def _attn_sample(q, k, v, cache_k, cache_v, bias, mult, layer, S, T):
    rows, aw = q.shape
    L = cache_k.shape[2]
    LP = bias.shape[-1]
    new = pl.BlockSpec((T, aw), lambda s: (s, 0))
    cache = pl.BlockSpec((None, None, L, aw), lambda s: (layer, s, 0, 0))
    return pl.pallas_call(
        functools.partial(_attn_sample_kernel, L=L, T=T, LP=LP),
        grid=(S,),
        in_specs=[new, new, new, cache, cache,
                  pl.BlockSpec(bias.shape, lambda s: (0, 0, 0)),
                  pl.BlockSpec((T, LP), lambda s: (0, 0))],
        out_specs=new,
        out_shape=jax.ShapeDtypeStruct((rows, aw), BF16),
        scratch_shapes=[pltpu.VMEM((LP, LANES), BF16), pltpu.VMEM((LP, LANES), BF16)],
        compiler_params=_cparams("parallel"),
        name="attn_sample",
    )(q, k, v, cache_k, cache_v, bias, mult)


def _outproj_kernel(h_ref, yc_ref, ya_ref, wo_ref, g_ref, wq_ref, h1_ref, xt_ref, pq_ref, *, conv_ch):
    mix = jnp.dot(yc_ref[...].astype(BF16), wo_ref[0:conv_ch, :], preferred_element_type=F32)
    mix = mix + jnp.dot(ya_ref[...], wo_ref[conv_ch:, :], preferred_element_type=F32)
    h1 = h_ref[...] + mix
    h1_ref[...] = h1
    xn = _rms(h1, g_ref[...])
    pq = jnp.dot(xn.astype(BF16), wq_ref[...], preferred_element_type=F32)
    hw = 2 * N_KEYS
    for hd in range(PEER_HEADS):
        pq_ref[hd] = pq[:, hd * hw:(hd + 1) * hw]
    xt_ref[...] = pltpu.bitcast(xn.T.astype(BF16), jnp.uint32)


def _outproj(h, yc, ya, wo_all, g_all, wq_all, layer):
    rows, d = h.shape
    conv_ch = yc.shape[1]
    nq = wq_all.shape[-1]
    tm = _row_tile(rows, 512)
    row = lambda w: pl.BlockSpec((tm, w), lambda i: (i, 0))
    return pl.pallas_call(
        functools.partial(_outproj_kernel, conv_ch=conv_ch),
        grid=(rows // tm,),
        in_specs=[row(d), row(conv_ch), row(ya.shape[1]),
                  pl.BlockSpec((None, d, d), lambda i: (layer, 0, 0)),
                  pl.BlockSpec((None, 1, d), lambda i: (layer, 0, 0)),
                  pl.BlockSpec((None, d, nq), lambda i: (layer, 0, 0))],
        out_specs=[row(d), pl.BlockSpec((d // 2, tm), lambda i: (0, i)),
                   pl.BlockSpec((PEER_HEADS, tm, nq // PEER_HEADS), lambda i: (0, i, 0))],
        out_shape=[jax.ShapeDtypeStruct((rows, d), F32),
                   jax.ShapeDtypeStruct((d // 2, rows), jnp.uint32),
                   jax.ShapeDtypeStruct((PEER_HEADS, rows, nq // PEER_HEADS), F32)],
        compiler_params=_cparams("parallel"),
        name="outproj",
    )(h, yc, ya, wo_all, g_all, wq_all)


def _top16_exact(x, pos):
    big = float(PEER_TOPK * PEER_TOPK)
    rank = jnp.full(x.shape, float(PEER_TOPK), F32)
    vals = []
    for r in range(PEER_TOPK):
        m = jnp.max(x, axis=0, keepdims=True)
        idx = jnp.min(jnp.where(x == m, pos, big), axis=0, keepdims=True)
        hit = pos == idx
        x = jnp.where(hit, -jnp.inf, x)
        rank = jnp.where(hit, float(r), rank)
        vals.append(m)
    return vals, rank


def _top16_distinct(x):
    rank = jnp.full(x.shape, float(PEER_TOPK), F32)
    vals = []
    for r in range(PEER_TOPK):
        m = jnp.max(x, axis=0, keepdims=True)
        hit = x == m
        x = jnp.where(hit, -jnp.inf, x)
        rank = jnp.where(hit, float(r), rank)
        vals.append(m)
    count = jnp.sum(jnp.where(rank < float(PEER_TOPK), 1.0, 0.0), axis=0, keepdims=True)
    return vals, rank, count


def _stack_rows(rows):
    n = len(rows)
    iota = lax.broadcasted_iota(jnp.int32, (n, rows[0].shape[1]), 0)
    out = jnp.zeros(iota.shape, F32)
    for r, v in enumerate(rows):
        out = jnp.where(iota == r, v, out)
    return out


_CAND_BLOCKS = ((0, 16), (1, 16), (2, 8), (3, 8), (4, 8), (5, 8), (6, 8), (7, 8))
_CAND_TAIL_R1 = 8


def _cand_pos():
    pos = []
    for r1, n2 in _CAND_BLOCKS:
        pos += [r1 * PEER_TOPK + r2 for r2 in range(n2)]
    pos += [r1 * PEER_TOPK for r1 in range(_CAND_TAIL_R1, PEER_TOPK)]
    return np.asarray(pos, np.float32)[:, None]


def _cand_build(a_rows, a_mat, b_mat, b_row0, op):
    parts = [op(a_rows[r1], b_mat[0:n2, :]) for r1, n2 in _CAND_BLOCKS]
    parts.append(op(a_mat[_CAND_TAIL_R1:, :], b_row0))
    return jnp.concatenate(parts, axis=0)


def _cand_counts(member):
    counts, off = [], 0
    for _, n2 in _CAND_BLOCKS:
        counts.append(jnp.sum(member[off:off + n2, :], axis=0, keepdims=True))
        off += n2
    for k in range(PEER_TOPK - _CAND_TAIL_R1):
        counts.append(member[off + k:off + k + 1, :])
    return counts


def _dup_bf16_words(x):
    b = pltpu.bitcast(x.astype(BF16).astype(F32), jnp.uint32)
    return b | (b >> 16)


def _peer_head_factors(s1, s2, pos_c, exact):
    if exact:
        key_pos = lax.broadcasted_iota(jnp.int32, s1.shape, 0).astype(F32)
        t1, rank1 = _top16_exact(s1, key_pos)
        t2, rank2 = _top16_exact(s2, key_pos)
        ok = None
    else:
        t1, rank1, n1 = _top16_distinct(s1)
        t2, rank2, n2 = _top16_distinct(s2)
    t1m, t2m = _stack_rows(t1), _stack_rows(t2)
    e1r = jnp.exp(t1m - t1[0])
    e2r = jnp.exp(t2m - t2[0])
    cand = _cand_build(t1, t1m, t2m, t2[0], lambda a, b: a + b)
    prod = _cand_build([e1r[r:r + 1, :] for r in range(PEER_TOPK)], e1r, e2r, e2r[0:1, :], lambda a, b: a * b)
    if exact:
        _, crank = _top16_exact(cand, pos_c)
        sel = crank < float(PEER_TOPK)
    else:
        x = cand
        for _ in range(PEER_TOPK):
            x = jnp.where(x == jnp.max(x, axis=0, keepdims=True), -jnp.inf, x)
        sel = x == -jnp.inf
    member = jnp.where(sel, 1.0, 0.0)
    if not exact:
        nc = jnp.sum(member, axis=0, keepdims=True)
        full = float(PEER_TOPK)
        ok = jnp.logical_and(jnp.logical_and(n1 == full, n2 == full), nc == full)
    z = jnp.sum(member * prod, axis=0, keepdims=True)
    counts = _cand_counts(member)
    lam = jnp.zeros(s1.shape, F32)
    for r in range(PEER_TOPK):
        lam = jnp.where(rank1 == float(r), counts[r], lam)
    e1z = jnp.exp(s1 - t1[0]) * (1.0 / z)
    e2 = jnp.exp(s2 - t2[0])
    return (lam, e1z, rank2, e2), ok


def _peer_topk_kernel(pq_ref, k1_ref, k2_ref, pos_ref, lam_ref, e1z_ref, rho_ref, e2_ref):
    k1 = k1_ref[...].astype(BF16)
    k2 = k2_ref[...].astype(BF16)
    pos_c = pos_ref[...]
    nt = (((1,), (1,)), ((), ()))

    def head(h, bad, exact):
        q = pq_ref[h]
        s1 = lax.dot_general(k1, q[:, :N_KEYS].astype(BF16), nt, preferred_element_type=F32)
        s2 = lax.dot_general(k2, q[:, N_KEYS:].astype(BF16), nt, preferred_element_type=F32)
        (lam, e1z, rho, e2), ok = _peer_head_factors(s1, s2, pos_c, exact)
        lam_ref[h] = _dup_bf16_words(lam)
        e1z_ref[h] = _dup_bf16_words(e1z)
        rho_ref[h] = pltpu.bitcast(rho.astype(BF16), jnp.uint32)
        e2_ref[h] = pltpu.bitcast(e2.astype(BF16), jnp.uint32)
        return bad if exact else jnp.maximum(bad, jnp.where(ok, 0.0, 1.0))

    bad = jnp.zeros((1, pq_ref.shape[1]), F32)
    for h in range(PEER_HEADS):
        bad = head(h, bad, exact=False)

    @pl.when(jnp.max(bad) > 0.0)
    def _():
        for h in range(PEER_HEADS):
            head(h, bad, exact=True)


def _peer_topk(pq, k1_all, k2_all, layer):
    _, rows, hw = pq.shape
    tt = LANES
    pos_c = jnp.asarray(np.broadcast_to(_cand_pos(), (_cand_pos().shape[0], tt)))
    blk = pl.BlockSpec((PEER_HEADS, N_KEYS, tt), lambda i: (0, 0, i))
    hblk = pl.BlockSpec((PEER_HEADS, N_KEYS // 2, tt), lambda i: (0, 0, i))
    kblk = pl.BlockSpec((None, N_KEYS, N_KEYS), lambda i: (layer, 0, 0))
    word = jax.ShapeDtypeStruct((PEER_HEADS, N_KEYS, rows), jnp.uint32)
    half = jax.ShapeDtypeStruct((PEER_HEADS, N_KEYS // 2, rows), jnp.uint32)
    return pl.pallas_call(
        _peer_topk_kernel,
        grid=(rows // tt,),
        in_specs=[pl.BlockSpec((PEER_HEADS, tt, hw), lambda i: (0, i, 0)), kblk, kblk,
                  pl.BlockSpec(pos_c.shape, lambda i: (0, 0))],
        out_specs=[blk, blk, hblk, hblk],
        out_shape=[word, word, half, half],
        compiler_params=_cparams("parallel"),
        name="peer_topk",
    )(pq, k1_all, k2_all, pos_c)


def _gelu(x):
    return 0.5 * x * (1.0 + lax.erf(x * (2.0 ** -0.5)))


def _pack_rows(x):
    return pltpu.bitcast(x.astype(BF16), jnp.uint32)


def _unpack_rows(w):
    return pltpu.bitcast(w, BF16)


def _vt_prep_kernel(v_ref, o_ref):
    o_ref[...] = _pack_rows(v_ref[...].T)


def _u_prep_kernel(u_ref, o_ref):
    o_ref[...] = _pack_rows(u_ref[...])


def _expert_prep(peer_u, peer_v):
    depth, n_exp, d = peer_u.shape
    eb = 1024
    src = pl.BlockSpec((None, eb, d), lambda l, e: (l, e, 0))
    u_w = pl.pallas_call(
        _u_prep_kernel,
        grid=(depth, n_exp // eb),
        in_specs=[src],
        out_specs=pl.BlockSpec((None, eb // 2, d), lambda l, e: (l, e, 0)),
        out_shape=jax.ShapeDtypeStruct((depth, n_exp // 2, d), jnp.uint32),
        compiler_params=_cparams("parallel", "parallel"),
        name="u_prep",
    )(peer_u)
    vt_w = pl.pallas_call(
        _vt_prep_kernel,
        grid=(depth, n_exp // eb),
        in_specs=[src],
        out_specs=pl.BlockSpec((None, d // 2, eb), lambda l, e: (l, 0, e)),
        out_shape=jax.ShapeDtypeStruct((depth, d // 2, n_exp), jnp.uint32),
        compiler_params=_cparams("parallel", "parallel"),
        name="vt_prep",
    )(peer_v)
    return u_w, vt_w


PACKED_ROWS = 16


def _peer_main_kernel(xt_ref, u_ref, vt_ref, lam_ref, e1z_ref, rho_ref, e2_ref, h1_ref, o_ref,
                      acc, c_sc, *, TT, EB):
    e = pl.program_id(1)
    half = PACKED_ROWS // 2

    @pl.when(e == 0)
    def _():
        acc[...] = jnp.zeros(acc.shape, F32)

    a = jnp.dot(_unpack_rows(u_ref[...]), _unpack_rows(xt_ref[...]), preferred_element_type=F32)
    for c in range(EB // N_KEYS):
        for tc in range(TT // LANES):
            cols = slice(tc * LANES, (tc + 1) * LANES)
            bcast = lambda ref, h: pltpu.bitcast(jnp.broadcast_to(ref[h, c:c + 1, cols], (half, LANES)), BF16)
            lam = [bcast(lam_ref, h) for h in range(PEER_HEADS)]
            e1z = [bcast(e1z_ref, h) for h in range(PEER_HEADS)]
            for jv in range(N_KEYS // PACKED_ROWS):
                wrows = slice(jv * half, (jv + 1) * half)
                w = jnp.zeros((PACKED_ROWS, LANES), BF16)
                for h in range(PEER_HEADS):
                    rho = pltpu.bitcast(rho_ref[h, wrows, cols], BF16)
                    e2 = pltpu.bitcast(e2_ref[h, wrows, cols], BF16)
                    w = w + jnp.where(rho < lam[h], e2 * e1z[h], 0.0)
                e0 = c * N_KEYS + jv * PACKED_ROWS
                act = _gelu(a[e0:e0 + PACKED_ROWS, cols]).astype(BF16)
                c_sc[e0 // 2:e0 // 2 + half, cols] = pltpu.bitcast(w * act, jnp.uint32)
    acc[...] += jnp.dot(_unpack_rows(vt_ref[...]), _unpack_rows(c_sc[...]), preferred_element_type=F32)

    @pl.when(e == pl.num_programs(1) - 1)
    def _():
        o_ref[...] = h1_ref[...] + acc[...].T


def _peer_main(xt, u_all, vt_all, lam, e1z, rho, e2, h1, layer):
    rows, d = h1.shape
    n_exp = vt_all.shape[2]
    tt = _row_tile(rows, 512)
    eb = 2048
    cb = eb // N_KEYS
    fac_c = pl.BlockSpec((PEER_HEADS, cb, tt), lambda i, e: (0, e, i))
    fac_all = pl.BlockSpec((PEER_HEADS, N_KEYS // 2, tt), lambda i, e: (0, 0, i))
    return pl.pallas_call(
        functools.partial(_peer_main_kernel, TT=tt, EB=eb),
        grid=(rows // tt, n_exp // eb),
        in_specs=[pl.BlockSpec((d // 2, tt), lambda i, e: (0, i)),
                  pl.BlockSpec((None, eb // 2, d), lambda i, e: (layer, e, 0)),
                  pl.BlockSpec((None, d // 2, eb), lambda i, e: (layer, 0, e)),
                  fac_c, fac_c, fac_all, fac_all,
                  pl.BlockSpec((tt, d), lambda i, e: (i, 0))],
        out_specs=pl.BlockSpec((tt, d), lambda i, e: (i, 0)),
        out_shape=jax.ShapeDtypeStruct((rows, d), F32),
        scratch_shapes=[pltpu.VMEM((d, tt), F32), pltpu.VMEM((eb // 2, tt), jnp.uint32)],
        compiler_params=_cparams("parallel", "arbitrary"),
        name="peer_main",
    )(xt, u_all, vt_all, lam, e1z, rho, e2, h1)


def _ple_kernel(h_ref, p_ref, g_ref, wg_ref, wp_ref, gf_ref, o_ref, *, final):
    h = h_ref[...]
    gate = jax.nn.sigmoid(jnp.dot(_rms(h, g_ref[...]).astype(BF16), wg_ref[...], preferred_element_type=F32))
    h = h + jnp.dot(p_ref[...].astype(BF16), wp_ref[...], preferred_element_type=F32) * gate
    o_ref[...] = _rms(h, gf_ref[...]) if final else h


def _ple(h, p_all, g_all, wg_all, wp_all, g_final, layer, final):
    rows, d = h.shape
    pd = p_all.shape[-1]
    tm = _row_tile(rows, 512)
    return pl.pallas_call(
        functools.partial(_ple_kernel, final=final),
        grid=(rows // tm,),
        in_specs=[pl.BlockSpec((tm, d), lambda i: (i, 0)),
                  pl.BlockSpec((None, tm, pd), lambda i: (layer, i, 0)),
                  pl.BlockSpec((None, 1, d), lambda i: (layer, 0, 0)),
                  pl.BlockSpec((None, d, d), lambda i: (layer, 0, 0)),
                  pl.BlockSpec((None, pd, d), lambda i: (layer, 0, 0)),
                  pl.BlockSpec((1, d), lambda i: (0, 0))],
        out_specs=pl.BlockSpec((tm, d), lambda i: (i, 0)),
        out_shape=jax.ShapeDtypeStruct((rows, d), F32),
        compiler_params=_cparams("parallel"),
        name="ple",
    )(h, p_all, g_all, wg_all, wp_all, g_final)


def kernel(x_prompt, x_sample, cache_k, cache_v, state_conv, p_prompt, p_sample, g_mix, w_in, conv_w, w_out,
           rel_bias, g_ffn, peer_wq, peer_k1, peer_k2, peer_u, peer_v, g_ple, w_ple_gate, w_ple_proj, g_final):
    B, T, D = x_prompt.shape
    S, TS, _ = x_sample.shape
    depth = w_in.shape[0]
    conv_ch = conv_w.shape[-1]
    attn_w = D - conv_ch
    n_heads = attn_w // HEAD_DIM
    L = cache_k.shape[2]
    keep = min(MAX_WINDOW, T)
    QB = 128
    assert all(w == BAND * d for w, d in DILATED_PATTERNS)
    assert T % (QB * max(d for _, d in DILATED_PATTERNS)) == 0 and attn_w % LANES == 0

    w_in_b = w_in.astype(BF16)
    w_out_b = w_out.astype(BF16)
    wq_b = peer_wq.astype(BF16)
    wg_b = w_ple_gate.astype(BF16)
    wp_b = w_ple_proj.astype(BF16)
    u_b, vt_b = _expert_prep(peer_u, peer_v)
    g_mix3, g_ffn3, g_ple3 = (g.reshape(depth, 1, D) for g in (g_mix, g_ffn, g_ple))
    g_fin2 = g_final.reshape(1, D)
    ck = cache_k.reshape(depth, S, L, attn_w).astype(BF16)
    cv = cache_v.reshape(depth, S, L, attn_w).astype(BF16)
    pp = p_prompt.reshape(depth, B * T, -1)
    ps = p_sample.reshape(depth, S * TS, -1)

    bias_rows = _prompt_bias_rows(rel_bias, QB + BAND)
    LP = L + LANES
    sbias, smult = _sample_tables(rel_bias, L, TS, LP)

    hp = x_prompt.reshape(B * T, D)
    hs = x_sample.reshape(S * TS, D)
    zero_past = jnp.zeros((B, 2, conv_ch), F32)
    kp_l, vp_l, cp_l, ks_l, vs_l, cs_l = [], [], [], [], [], []

    def mixer_tail(h, yc, ya, lam_src_layer):
        i = lam_src_layer
        h1, xt, pq = _outproj(h, yc, ya, w_out_b, g_ffn3, wq_b, i)
        lam, e1z, rho, e2 = _peer_topk(pq, peer_k1, peer_k2, i)
        h2 = _peer_main(xt, u_b, vt_b, lam, e1z, rho, e2, h1, i)
        return h2

    for i in range(depth):
        final = i == depth - 1
        u, bg, q, k, v = _inproj(hp, g_mix3, w_in_b, i, conv_ch, attn_w)
        yc, cst = _conv(u, bg, zero_past, conv_w, i, B, T)
        ya = _attn_prompt(q, k, v, bias_rows, B, T, QB)
        h2 = mixer_tail(hp, yc, ya, i)
        hp = _ple(h2, pp, g_ple3, wg_b, wp_b, g_fin2, i, final)
        kp_l.append(k.reshape(B, T, n_heads, HEAD_DIM)[:, T - keep:])
        vp_l.append(v.reshape(B, T, n_heads, HEAD_DIM)[:, T - keep:])
        cp_l.append(cst)
        u, bg, q, k, v = _inproj(hs, g_mix3, w_in_b, i, conv_ch, attn_w)
        yc, cst = _conv(u, bg, state_conv[i], conv_w, i, S, TS)
        ya = _attn_sample(q, k, v, ck, cv, sbias, smult, i, S, TS)
        h2 = mixer_tail(hs, yc, ya, i)
        hs = _ple(h2, ps, g_ple3, wg_b, wp_b, g_fin2, i, final)
        ks_l.append(k.reshape(S, TS, n_heads, HEAD_DIM))
        vs_l.append(v.reshape(S, TS, n_heads, HEAD_DIM))
        cs_l.append(cst)

    return (hp.reshape(B, T, D), hs.reshape(S, TS, D), jnp.stack(kp_l), jnp.stack(vp_l), jnp.stack(cp_l),
            jnp.stack(ks_l), jnp.stack(vs_l), jnp.stack(cs_l))
```

```python
import functools
import math

import numpy as np
import jax
import jax.numpy as jnp
from jax import lax
from jax.experimental import pallas as pl
from jax.experimental.pallas import tpu as pltpu

F32 = jnp.float32
BF16 = jnp.bfloat16

HEAD_DIM = 64
LANES = 128
DILATED_PATTERNS = ((128, 1), (512, 4), (2048, 16))
BAND = 128
MAX_WINDOW = 2048
N_BUCKETS = 32
MAX_DISTANCE = 2048
N_KEYS = 128
PEER_HEADS = 8
PEER_TOPK = 16
EPS = 1e-6
NEG = -1e30
VMEM_LIMIT_BYTES = 56 * 1024 * 1024


def _cparams(*sem):
    return pltpu.CompilerParams(dimension_semantics=sem, vmem_limit_bytes=VMEM_LIMIT_BYTES)


def _row_tile(rows, cap):
    t = min(rows, cap)
    while rows % t:
        t //= 2
    return t


def _rms(x, g):
    return x * lax.rsqrt(jnp.mean(x * x, axis=-1, keepdims=True) + EPS) * g


def _inproj_kernel(x_ref, g_ref, w_ref, u_ref, bg_ref, q_ref, k_ref, v_ref, *, conv_ch, attn_w):
    xn = _rms(x_ref[...], g_ref[...])
    proj = jnp.dot(xn.astype(BF16), w_ref[...], preferred_element_type=F32)
    c, a = conv_ch, attn_w
    u_ref[...] = proj[:, 2 * c:3 * c] * proj[:, :c]
    bg_ref[...] = proj[:, c:2 * c]
    q_ref[...] = proj[:, 3 * c:3 * c + a]
    k_ref[...] = proj[:, 3 * c + a:3 * c + 2 * a]
    v_ref[...] = proj[:, 3 * c + 2 * a:]


def _inproj(x, g_all, w_all, layer, conv_ch, attn_w):
    rows, d = x.shape
    tm = _row_tile(rows, 512)
    n_out = w_all.shape[-1]
    row = lambda w: pl.BlockSpec((tm, w), lambda i: (i, 0))
    return pl.pallas_call(
        functools.partial(_inproj_kernel, conv_ch=conv_ch, attn_w=attn_w),
        grid=(rows // tm,),
        in_specs=[row(d),
                  pl.BlockSpec((None, 1, d), lambda i: (layer, 0, 0)),
                  pl.BlockSpec((None, d, n_out), lambda i: (layer, 0, 0))],
        out_specs=[row(conv_ch), row(conv_ch), row(attn_w), row(attn_w), row(attn_w)],
        out_shape=[jax.ShapeDtypeStruct((rows, conv_ch), F32)] * 2
        + [jax.ShapeDtypeStruct((rows, attn_w), F32)] * 3,
        compiler_params=_cparams("parallel"),
        name="inproj",
    )(x, g_all, w_all)


def _conv_kernel(u_ref, bg_ref, past_ref, w_ref, y_ref, st_ref, *, T):
    u = u_ref[...]
    row = lax.broadcasted_iota(jnp.int32, u.shape, 0)
    p0 = past_ref[0:1, :]
    p1 = past_ref[1:2, :]
    um1 = jnp.where(row == 0, p1, pltpu.roll(u, 1, 0))
    um2 = jnp.where(row == 0, p0, jnp.where(row == 1, p1, pltpu.roll(u, 2, 0)))
    conv = w_ref[0:1, :] * um2 + w_ref[1:2, :] * um1 + w_ref[2:3, :] * u
    y_ref[...] = bg_ref[...] * conv
    st_ref[...] = u_ref[T - 2:T, :]


def _conv(u, bg, past, w_all, layer, nseq, T):
    rows, c = u.shape
    blk = pl.BlockSpec((T, c), lambda s: (s, 0))
    return pl.pallas_call(
        functools.partial(_conv_kernel, T=T),
        grid=(nseq,),
        in_specs=[blk, blk,
                  pl.BlockSpec((None, 2, c), lambda s: (s, 0, 0)),
                  pl.BlockSpec((None, 3, c), lambda s: (layer, 0, 0))],
        out_specs=[blk, pl.BlockSpec((None, 2, c), lambda s: (s, 0, 0))],
        out_shape=[jax.ShapeDtypeStruct((rows, c), F32),
                   jax.ShapeDtypeStruct((nseq, 2, c), F32)],
        compiler_params=_cparams("parallel"),
        name="conv",
    )(u, bg, past, w_all)


def _t5_bucket_np(dist):
    max_exact = N_BUCKETS // 2
    df = np.maximum(dist, 1).astype(np.float32)
    large = max_exact + (np.log(df / np.float32(max_exact)) / np.float32(math.log(MAX_DISTANCE / max_exact))
                         * np.float32(N_BUCKETS - max_exact)).astype(np.int32)
    large = np.minimum(large, N_BUCKETS - 1)
    return np.where(dist < max_exact, dist, large).astype(np.int32)


def _static_rows(table, idx):
    onehot = jnp.asarray(np.eye(table.shape[0], dtype=np.float32)[idx])
    return jnp.einsum('...k,kh->...h', onehot, table, precision=lax.Precision.HIGHEST)


def _prompt_bias_rows(rel_bias, kb):
    m = np.arange(kb)
    idx = np.full((len(DILATED_PATTERNS), kb), N_BUCKETS, np.int32)
    for r, (_, dil) in enumerate(DILATED_PATTERNS):
        idx[r] = np.where(m <= BAND, _t5_bucket_np(np.clip(BAND - m, 0, BAND) * dil), N_BUCKETS)
    ext = jnp.concatenate([rel_bias.astype(F32), jnp.full((1, rel_bias.shape[1]), NEG, F32)], axis=0)
    rows = jnp.transpose(_static_rows(ext, idx), (0, 2, 1))
    return rows.reshape(rows.shape[0], rows.shape[1] // 2, 2, kb)


def _sample_tables(rel_bias, L, T, lp):
    t = np.arange(T)[:, None]
    pos = np.arange(lp)[None, :]
    dist = L + t - pos
    valid = (dist >= 0) & (pos < L + T)
    mult = np.zeros((T, lp), np.float32)
    for window, dil in DILATED_PATTERNS:
        mult += (valid & (dist % dil == 0) & (dist <= window)).astype(np.float32)
    idx = _t5_bucket_np(np.clip(dist, 0, MAX_DISTANCE))
    bias = jnp.transpose(_static_rows(rel_bias.astype(F32), idx), (2, 0, 1))
    return bias, jnp.asarray(mult)


MERGE_ROWS = 256


def _attn_prompt_kernel(q_ref, k_ref, v_ref, brow_ref, o_ref, kp, vp, tab, m_sc, l_sc, acc_sc, *, T, QB, PAD):
    KB = QB + BAND
    npat = len(DILATED_PATTERNS)
    zeros = jnp.zeros((PAD, LANES), F32)
    kp[0:PAD, :] = zeros
    vp[0:PAD, :] = zeros
    kp[PAD:PAD + T, :] = k_ref[...]
    vp[PAD:PAD + T, :] = v_ref[...]
    before_start = lax.broadcasted_iota(jnp.int32, (QB, KB), 1) < BAND
    for r in range(npat):
        for hh in range(2):
            row = jnp.broadcast_to(brow_ref[r, hh:hh + 1, :], (QB, KB))
            toe = pltpu.roll(row, 0, 1, stride=1, stride_axis=0)
            tab[r, 0, hh * QB:(hh + 1) * QB, :] = toe
            tab[r, 1, hh * QB:(hh + 1) * QB, :] = jnp.where(before_start, NEG, toe)
    is0 = lax.broadcasted_iota(jnp.int32, (1, LANES), 1) < HEAD_DIM
    scale = HEAD_DIM ** -0.5

    for r, (_, d) in enumerate(DILATED_PATTERNS):
        nblk = (T // d) // QB

        def body(idx, carry, r=r, d=d, nblk=nblk):
            c = idx // nblk
            bi = idx % nblk
            a0 = bi * QB
            qstart = c + d * a0
            kstart = PAD + c + d * (a0 - BAND)
            if d == 1:
                qsl = pl.ds(pl.multiple_of(qstart, QB), QB)
                ksl = pl.ds(pl.multiple_of(kstart, BAND), KB)
            else:
                qsl = pl.ds(qstart, QB, stride=d)
                ksl = pl.ds(kstart, KB, stride=d)
            q = q_ref[qsl, :] * scale
            kb = kp[ksl, :].astype(BF16)
            vb = vp[ksl, :].astype(BF16)
            variant = jnp.where(bi == 0, 1, 0)
            q2 = jnp.concatenate([jnp.where(is0, q, 0.0), jnp.where(is0, 0.0, q)], axis=0).astype(BF16)
            sc = lax.dot_general(q2, kb, (((1,), (1,)), ((), ())), preferred_element_type=F32)
            sc = sc + tab[r, variant]
            m = jnp.max(sc, axis=1, keepdims=True)
            p = jnp.exp(sc - m)
            l = jnp.sum(p, axis=1, keepdims=True)
            pv = jnp.dot(p.astype(BF16), vb, preferred_element_type=F32)
            m_sc[r, qsl, :] = jnp.where(is0, m[:QB], m[QB:])
            l_sc[r, qsl, :] = jnp.where(is0, l[:QB], l[QB:])
            acc_sc[r, qsl, :] = jnp.where(is0, pv[:QB], pv[QB:])
            return carry

        lax.fori_loop(0, d * nblk, body, 0, unroll=True)

    def merge(i, carry):
        rows = pl.ds(pl.multiple_of(i * MERGE_ROWS, MERGE_ROWS), MERGE_ROWS)
        m = [m_sc[r, rows, :] for r in range(npat)]
        mx = functools.reduce(jnp.maximum, m)
        w = [jnp.exp(mr - mx) for mr in m]
        num = sum(w[r] * acc_sc[r, rows, :] for r in range(npat))
        den = sum(w[r] * l_sc[r, rows, :] for r in range(npat))
        o_ref[rows, :] = (num / den).astype(o_ref.dtype)
        return carry

    lax.fori_loop(0, T // MERGE_ROWS, merge, 0)


def _attn_prompt(q, k, v, bias_rows, B, T, QB):
    rows, aw = q.shape
    npair = aw // LANES
    npat = len(DILATED_PATTERNS)
    PAD = BAND * max(d for _, d in DILATED_PATTERNS)
    KB = QB + BAND
    blk = pl.BlockSpec((T, LANES), lambda b, hp: (b, hp))
    return pl.pallas_call(
        functools.partial(_attn_prompt_kernel, T=T, QB=QB, PAD=PAD),
        grid=(B, npair),
        in_specs=[blk, blk, blk,
                  pl.BlockSpec((npat, None, 2, KB), lambda b, hp: (0, hp, 0, 0))],
        out_specs=blk,
        out_shape=jax.ShapeDtypeStruct((rows, aw), BF16),
        scratch_shapes=[pltpu.VMEM((PAD + T, LANES), F32), pltpu.VMEM((PAD + T, LANES), F32),
                        pltpu.VMEM((npat, 2, 2 * QB, KB), F32),
                        pltpu.VMEM((npat, T, LANES), F32), pltpu.VMEM((npat, T, LANES), F32),
                        pltpu.VMEM((npat, T, LANES), F32)],
        compiler_params=_cparams("parallel", "parallel"),
        name="attn_prompt",
    )(q, k, v, bias_rows)


def _attn_sample_kernel(q_ref, k_ref, v_ref, ck_ref, cv_ref, bias_ref, mult_ref, o_ref, kf, vf, *, L, T, LP):
    is0 = lax.broadcasted_iota(jnp.int32, (1, LANES), 1) < HEAD_DIM
    mult = mult_ref[...]
    pad = jnp.zeros((LP - L, LANES), BF16)
    for hp in range(q_ref.shape[1] // LANES):
        lanes = slice(hp * LANES, (hp + 1) * LANES)
        kf[0:L, :] = ck_ref[:, lanes]
        vf[0:L, :] = cv_ref[:, lanes]
        kf[L:LP, :] = pad
        vf[L:LP, :] = pad
        kf[L:L + T, :] = k_ref[:, lanes].astype(BF16)
        vf[L:L + T, :] = v_ref[:, lanes].astype(BF16)
        q = q_ref[:, lanes] * (HEAD_DIM ** -0.5)
        kb = kf[...]
        vb = vf[...]
        outs = []
        for hh in range(2):
            qh = jnp.where(is0 if hh == 0 else jnp.logical_not(is0), q, 0.0).astype(BF16)
            s = lax.dot_general(qh, kb, (((1,), (1,)), ((), ())), preferred_element_type=F32)
            s = jnp.where(mult > 0.0, s + bias_ref[2 * hp + hh], NEG)
            m = jnp.max(s, axis=1, keepdims=True)
            p = jnp.exp(s - m) * mult
            l = jnp.sum(p, axis=1, keepdims=True)
            outs.append(jnp.dot(p.astype(BF16), vb, preferred_element_type=F32) / l)
        o_ref[:, lanes] = jnp.where(is0, outs[0], outs[1]).astype(o_ref.dtype)


def _attn_sample(q, k, v, cache_k, cache_v, bias, mult, layer, S, T):
    rows, aw = q.shape
    L = cache_k.shape[2]
    LP = bias.shape[-1]
    new = pl.BlockSpec((T, aw), lambda s: (s, 0))
    cache = pl.BlockSpec((None, None, L, aw), lambda s: (layer, s, 0, 0))
    return pl.pallas_call(
        functools.partial(_attn_sample_kernel, L=L, T=T, LP=LP),
        grid=(S,),
        in_specs=[new, new, new, cache, cache,
                  pl.BlockSpec(bias.shape, lambda s: (0, 0, 0)),
                  pl.BlockSpec((T, LP), lambda s: (0, 0))],
        out_specs=new,
        out_shape=jax.ShapeDtypeStruct((rows, aw), BF16),
        scratch_shapes=[pltpu.VMEM((LP, LANES), BF16), pltpu.VMEM((LP, LANES), BF16)],
        compiler_params=_cparams("parallel"),
        name="attn_sample",
    )(q, k, v, cache_k, cache_v, bias, mult)


def _outproj_kernel(h_ref, yc_ref, ya_ref, wo_ref, g_ref, wq_ref, h1_ref, xt_ref, pq_ref, *, conv_ch):
    mix = jnp.dot(yc_ref[...].astype(BF16), wo_ref[0:conv_ch, :], preferred_element_type=F32)
    mix = mix + jnp.dot(ya_ref[...], wo_ref[conv_ch:, :], preferred_element_type=F32)
    h1 = h_ref[...] + mix
    h1_ref[...] = h1
    xn = _rms(h1, g_ref[...])
    pq = jnp.dot(xn.astype(BF16), wq_ref[...], preferred_element_type=F32)
    hw = 2 * N_KEYS
    for hd in range(PEER_HEADS):
        pq_ref[hd] = pq[:, hd * hw:(hd + 1) * hw]
    xt_ref[...] = pltpu.bitcast(xn.T.astype(BF16), jnp.uint32)


def _outproj(h, yc, ya, wo_all, g_all, wq_all, layer):
    rows, d = h.shape
    conv_ch = yc.shape[1]
    nq = wq_all.shape[-1]
    tm = _row_tile(rows, 512)
    row = lambda w: pl.BlockSpec((tm, w), lambda i: (i, 0))
    return pl.pallas_call(
        functools.partial(_outproj_kernel, conv_ch=conv_ch),
        grid=(rows // tm,),
        in_specs=[row(d), row(conv_ch), row(ya.shape[1]),
                  pl.BlockSpec((None, d, d), lambda i: (layer, 0, 0)),
                  pl.BlockSpec((None, 1, d), lambda i: (layer, 0, 0)),
                  pl.BlockSpec((None, d, nq), lambda i: (layer, 0, 0))],
        out_specs=[row(d), pl.BlockSpec((d // 2, tm), lambda i: (0, i)),
                   pl.BlockSpec((PEER_HEADS, tm, nq // PEER_HEADS), lambda i: (0, i, 0))],
        out_shape=[jax.ShapeDtypeStruct((rows, d), F32),
                   jax.ShapeDtypeStruct((d // 2, rows), jnp.uint32),
                   jax.ShapeDtypeStruct((PEER_HEADS, rows, nq // PEER_HEADS), F32)],
        compiler_params=_cparams("parallel"),
        name="outproj",
    )(h, yc, ya, wo_all, g_all, wq_all)


def _top16_exact(x, pos):
    big = float(PEER_TOPK * PEER_TOPK)
    rank = jnp.full(x.shape, float(PEER_TOPK), F32)
    vals = []
    for r in range(PEER_TOPK):
        m = jnp.max(x, axis=0, keepdims=True)
        idx = jnp.min(jnp.where(x == m, pos, big), axis=0, keepdims=True)
        hit = pos == idx
        x = jnp.where(hit, -jnp.inf, x)
        rank = jnp.where(hit, float(r), rank)
        vals.append(m)
    return vals, rank


def _top16_distinct(x):
    rank = jnp.full(x.shape, float(PEER_TOPK), F32)
    vals = []
    for r in range(PEER_TOPK):
        m = jnp.max(x, axis=0, keepdims=True)
        hit = x == m
        x = jnp.where(hit, -jnp.inf, x)
        rank = jnp.where(hit, float(r), rank)
        vals.append(m)
    count = jnp.sum(jnp.where(rank < float(PEER_TOPK), 1.0, 0.0), axis=0, keepdims=True)
    return vals, rank, count


def _stack_rows(rows):
    n = len(rows)
    iota = lax.broadcasted_iota(jnp.int32, (n, rows[0].shape[1]), 0)
    out = jnp.zeros(iota.shape, F32)
    for r, v in enumerate(rows):
        out = jnp.where(iota == r, v, out)
    return out


_CAND_BLOCKS = ((0, 16), (1, 16), (2, 8), (3, 8), (4, 8), (5, 8), (6, 8), (7, 8))
_CAND_TAIL_R1 = 8


def _cand_pos():
    pos = []
    for r1, n2 in _CAND_BLOCKS:
        pos += [r1 * PEER_TOPK + r2 for r2 in range(n2)]
    pos += [r1 * PEER_TOPK for r1 in range(_CAND_TAIL_R1, PEER_TOPK)]
    return np.asarray(pos, np.float32)[:, None]


def _cand_build(a_rows, a_mat, b_mat, b_row0, op):
    parts = [op(a_rows[r1], b_mat[0:n2, :]) for r1, n2 in _CAND_BLOCKS]
    parts.append(op(a_mat[_CAND_TAIL_R1:, :], b_row0))
    return jnp.concatenate(parts, axis=0)


def _cand_counts(member):
    counts, off = [], 0
    for _, n2 in _CAND_BLOCKS:
        counts.append(jnp.sum(member[off:off + n2, :], axis=0, keepdims=True))
        off += n2
    for k in range(PEER_TOPK - _CAND_TAIL_R1):
        counts.append(member[off + k:off + k + 1, :])
    return counts


def _peer_head_factors(s1, s2, pos_c, exact):
    if exact:
        key_pos = lax.broadcasted_iota(jnp.int32, s1.shape, 0).astype(F32)
        t1, rank1 = _top16_exact(s1, key_pos)
        t2, rank2 = _top16_exact(s2, key_pos)
        ok = None
    else:
        t1, rank1, n1 = _top16_distinct(s1)
        t2, rank2, n2 = _top16_distinct(s2)
    t1m, t2m = _stack_rows(t1), _stack_rows(t2)
    e1r = jnp.exp(t1m - t1[0])
    e2r = jnp.exp(t2m - t2[0])
    cand = _cand_build(t1, t1m, t2m, t2[0], lambda a, b: a + b)
    prod = _cand_build([e1r[r:r + 1, :] for r in range(PEER_TOPK)], e1r, e2r, e2r[0:1, :], lambda a, b: a * b)
    if exact:
        _, crank = _top16_exact(cand, pos_c)
        sel = crank < float(PEER_TOPK)
    else:
        x = cand
        for _ in range(PEER_TOPK):
            x = jnp.where(x == jnp.max(x, axis=0, keepdims=True), -jnp.inf, x)
        sel = x == -jnp.inf
    member = jnp.where(sel, 1.0, 0.0)
    if not exact:
        nc = jnp.sum(member, axis=0, keepdims=True)
        full = float(PEER_TOPK)
        ok = jnp.logical_and(jnp.logical_and(n1 == full, n2 == full), nc == full)
    z = jnp.sum(member * prod, axis=0, keepdims=True)
    counts = _cand_counts(member)
    lam = jnp.zeros(s1.shape, F32)
    for r in range(PEER_TOPK):
        lam = jnp.where(rank1 == float(r), counts[r], lam)
    e1z = jnp.exp(s1 - t1[0]) * (1.0 / z)
    e2 = jnp.exp(s2 - t2[0])
    return (lam, e1z, rank2, e2), ok


def _peer_topk_kernel(pq_ref, k1_ref, k2_ref, pos_ref, lam_ref, e1z_ref, rho_ref, e2_ref):
    k1 = k1_ref[...].astype(BF16)
    k2 = k2_ref[...].astype(BF16)
    pos_c = pos_ref[...]
    nt = (((1,), (1,)), ((), ()))

    def head(h, bad, exact):
        q = pq_ref[h]
        s1 = lax.dot_general(k1, q[:, :N_KEYS].astype(BF16), nt, preferred_element_type=F32)
        s2 = lax.dot_general(k2, q[:, N_KEYS:].astype(BF16), nt, preferred_element_type=F32)
        (lam, e1z, rho, e2), ok = _peer_head_factors(s1, s2, pos_c, exact)
        lam_ref[h] = lam
        e1z_ref[h] = e1z
        rho_ref[h] = pltpu.bitcast(rho.astype(BF16), jnp.uint32)
        e2_ref[h] = pltpu.bitcast(e2.astype(BF16), jnp.uint32)
        return bad if exact else jnp.maximum(bad, jnp.where(ok, 0.0, 1.0))

    bad = jnp.zeros((1, pq_ref.shape[1]), F32)
    for h in range(PEER_HEADS):
        bad = head(h, bad, exact=False)

    @pl.when(jnp.max(bad) > 0.0)
    def _():
        for h in range(PEER_HEADS):
            head(h, bad, exact=True)


def _peer_topk(pq, k1_all, k2_all, layer):
    _, rows, hw = pq.shape
    tt = LANES
    pos_c = jnp.asarray(np.broadcast_to(_cand_pos(), (_cand_pos().shape[0], tt)))
    blk = pl.BlockSpec((PEER_HEADS, N_KEYS, tt), lambda i: (0, 0, i))
    hblk = pl.BlockSpec((PEER_HEADS, N_KEYS // 2, tt), lambda i: (0, 0, i))
    kblk = pl.BlockSpec((None, N_KEYS, N_KEYS), lambda i: (layer, 0, 0))
    word = jax.ShapeDtypeStruct((PEER_HEADS, N_KEYS, rows), F32)
    half = jax.ShapeDtypeStruct((PEER_HEADS, N_KEYS // 2, rows), jnp.uint32)
    return pl.pallas_call(
        _peer_topk_kernel,
        grid=(rows // tt,),
        in_specs=[pl.BlockSpec((PEER_HEADS, tt, hw), lambda i: (0, i, 0)), kblk, kblk,
                  pl.BlockSpec(pos_c.shape, lambda i: (0, 0))],
        out_specs=[blk, blk, hblk, hblk],
        out_shape=[word, word, half, half],
        compiler_params=_cparams("parallel"),
        name="peer_topk",
    )(pq, k1_all, k2_all, pos_c)


def _gelu(x):
    return 0.5 * x * (1.0 + lax.erf(x * (2.0 ** -0.5)))


def _pack_rows(x):
    return pltpu.bitcast(x.astype(BF16), jnp.uint32)


def _unpack_rows(w):
    return pltpu.bitcast(w, BF16)


def _vt_prep_kernel(v_ref, o_ref):
    o_ref[...] = _pack_rows(v_ref[...].T)


def _u_prep_kernel(u_ref, o_ref):
    o_ref[...] = _pack_rows(u_ref[...])


def _expert_prep(peer_u, peer_v):
    depth, n_exp, d = peer_u.shape
    eb = 1024
    src = pl.BlockSpec((None, eb, d), lambda l, e: (l, e, 0))
    u_w = pl.pallas_call(
        _u_prep_kernel,
        grid=(depth, n_exp // eb),
        in_specs=[src],
        out_specs=pl.BlockSpec((None, eb // 2, d), lambda l, e: (l, e, 0)),
        out_shape=jax.ShapeDtypeStruct((depth, n_exp // 2, d), jnp.uint32),
        compiler_params=_cparams("parallel", "parallel"),
        name="u_prep",
    )(peer_u)
    vt_w = pl.pallas_call(
        _vt_prep_kernel,
        grid=(depth, n_exp // eb),
        in_specs=[src],
        out_specs=pl.BlockSpec((None, d // 2, eb), lambda l, e: (l, 0, e)),
        out_shape=jax.ShapeDtypeStruct((depth, d // 2, n_exp), jnp.uint32),
        compiler_params=_cparams("parallel", "parallel"),
        name="vt_prep",
    )(peer_v)
    return u_w, vt_w


PACKED_ROWS = 16
PIECE_SLABS = 2


def _peer_main_kernel(xt_ref, u_ref, vt_ref, lam_ref, e1z_ref, rho_ref, e2_ref, h1_ref, o_ref,
                      acc, c_sc, *, TT, EB):
    e = pl.program_id(1)
    half = PACKED_ROWS // 2

    @pl.when(e == 0)
    def _():
        acc[...] = jnp.zeros(acc.shape, F32)

    xt = _unpack_rows(xt_ref[...])
    piece_words = PIECE_SLABS * N_KEYS // 2
    for cp in range(EB // N_KEYS // PIECE_SLABS):
        a = jnp.dot(_unpack_rows(u_ref[cp * piece_words:(cp + 1) * piece_words, :]), xt,
                    preferred_element_type=F32)
        for ci in range(PIECE_SLABS):
            c = cp * PIECE_SLABS + ci
            for tc in range(TT // LANES):
                cols = slice(tc * LANES, (tc + 1) * LANES)
                bcast = lambda ref, h: jnp.broadcast_to(ref[h, c:c + 1, cols], (PACKED_ROWS, LANES)).astype(BF16)
                lam = [bcast(lam_ref, h) for h in range(PEER_HEADS)]
                e1z = [bcast(e1z_ref, h) for h in range(PEER_HEADS)]
                for jv in range(N_KEYS // PACKED_ROWS):
                    wrows = slice(jv * half, (jv + 1) * half)
                    w = jnp.zeros((PACKED_ROWS, LANES), BF16)
                    for h in range(PEER_HEADS):
                        rho = pltpu.bitcast(rho_ref[h, wrows, cols], BF16)
                        e2 = pltpu.bitcast(e2_ref[h, wrows, cols], BF16)
                        w = w + jnp.where(rho < lam[h], e2 * e1z[h], 0.0)
                    e0 = c * N_KEYS + jv * PACKED_ROWS
                    a0 = ci * N_KEYS + jv * PACKED_ROWS
                    act = _gelu(a[a0:a0 + PACKED_ROWS, cols]).astype(BF16)
                    c_sc[e0 // 2:e0 // 2 + half, cols] = pltpu.bitcast(w * act, jnp.uint32)
    acc[...] += jnp.dot(_unpack_rows(vt_ref[...]), _unpack_rows(c_sc[...]), preferred_element_type=F32)

    @pl.when(e == pl.num_programs(1) - 1)
    def _():
        o_ref[...] = h1_ref[...] + acc[...].T


def _peer_main(xt, u_all, vt_all, lam, e1z, rho, e2, h1, layer):
    rows, d = h1.shape
    n_exp = vt_all.shape[2]
    tt = _row_tile(rows, 512)
    eb = 2048
    cb = eb // N_KEYS
    fac_c = pl.BlockSpec((PEER_HEADS, cb, tt), lambda i, e: (0, e, i))
    fac_all = pl.BlockSpec((PEER_HEADS, N_KEYS // 2, tt), lambda i, e: (0, 0, i))
    return pl.pallas_call(
        functools.partial(_peer_main_kernel, TT=tt, EB=eb),
        grid=(rows // tt, n_exp // eb),
        in_specs=[pl.BlockSpec((d // 2, tt), lambda i, e: (0, i)),
                  pl.BlockSpec((None, eb // 2, d), lambda i, e: (layer, e, 0)),
                  pl.BlockSpec((None, d // 2, eb), lambda i, e: (layer, 0, e)),
                  fac_c, fac_c, fac_all, fac_all,
                  pl.BlockSpec((tt, d), lambda i, e: (i, 0))],
        out_specs=pl.BlockSpec((tt, d), lambda i, e: (i, 0)),
        out_shape=jax.ShapeDtypeStruct((rows, d), F32),
        scratch_shapes=[pltpu.VMEM((d, tt), F32), pltpu.VMEM((eb // 2, tt), jnp.uint32)],
        compiler_params=_cparams("parallel", "arbitrary"),
        name="peer_main",
    )(xt, u_all, vt_all, lam, e1z, rho, e2, h1)


def _ple_kernel(h_ref, p_ref, g_ref, wg_ref, wp_ref, gf_ref, o_ref, *, final):
    h = h_ref[...]
    gate = jax.nn.sigmoid(jnp.dot(_rms(h, g_ref[...]).astype(BF16), wg_ref[...], preferred_element_type=F32))
    h = h + jnp.dot(p_ref[...].astype(BF16), wp_ref[...], preferred_element_type=F32) * gate
    o_ref[...] = _rms(h, gf_ref[...]) if final else h


def _ple(h, p_all, g_all, wg_all, wp_all, g_final, layer, final):
    rows, d = h.shape
    pd = p_all.shape[-1]
    tm = _row_tile(rows, 512)
    return pl.pallas_call(
        functools.partial(_ple_kernel, final=final),
        grid=(rows // tm,),
        in_specs=[pl.BlockSpec((tm, d), lambda i: (i, 0)),
                  pl.BlockSpec((None, tm, pd), lambda i: (layer, i, 0)),
                  pl.BlockSpec((None, 1, d), lambda i: (layer, 0, 0)),
                  pl.BlockSpec((None, d, d), lambda i: (layer, 0, 0)),
                  pl.BlockSpec((None, pd, d), lambda i: (layer, 0, 0)),
                  pl.BlockSpec((1, d), lambda i: (0, 0))],
        out_specs=pl.BlockSpec((tm, d), lambda i: (i, 0)),
        out_shape=jax.ShapeDtypeStruct((rows, d), F32),
        compiler_params=_cparams("parallel"),
        name="ple",
    )(h, p_all, g_all, wg_all, wp_all, g_final)


def kernel(x_prompt, x_sample, cache_k, cache_v, state_conv, p_prompt, p_sample, g_mix, w_in, conv_w, w_out,
           rel_bias, g_ffn, peer_wq, peer_k1, peer_k2, peer_u, peer_v, g_ple, w_ple_gate, w_ple_proj, g_final):
    B, T, D = x_prompt.shape
    S, TS, _ = x_sample.shape
    depth = w_in.shape[0]
    conv_ch = conv_w.shape[-1]
    attn_w = D - conv_ch
    n_heads = attn_w // HEAD_DIM
    L = cache_k.shape[2]
    keep = min(MAX_WINDOW, T)
    QB = 128
    assert all(w == BAND * d for w, d in DILATED_PATTERNS)
    assert T % (QB * max(d for _, d in DILATED_PATTERNS)) == 0 and attn_w % LANES == 0

    w_in_b = w_in.astype(BF16)
    w_out_b = w_out.astype(BF16)
    wq_b = peer_wq.astype(BF16)
    wg_b = w_ple_gate.astype(BF16)
    wp_b = w_ple_proj.astype(BF16)
    u_b, vt_b = _expert_prep(peer_u, peer_v)
    g_mix3, g_ffn3, g_ple3 = (g.reshape(depth, 1, D) for g in (g_mix, g_ffn, g_ple))
    g_fin2 = g_final.reshape(1, D)
    ck = cache_k.reshape(depth, S, L, attn_w).astype(BF16)
    cv = cache_v.reshape(depth, S, L, attn_w).astype(BF16)
    pp = p_prompt.reshape(depth, B * T, -1)
    ps = p_sample.reshape(depth, S * TS, -1)

    bias_rows = _prompt_bias_rows(rel_bias, QB + BAND)
    LP = L + LANES
    sbias, smult = _sample_tables(rel_bias, L, TS, LP)

    hp = x_prompt.reshape(B * T, D)
    hs = x_sample.reshape(S * TS, D)
    zero_past = jnp.zeros((B, 2, conv_ch), F32)
    kp_l, vp_l, cp_l, ks_l, vs_l, cs_l = [], [], [], [], [], []

    def mixer_tail(h, yc, ya, lam_src_layer):
        i = lam_src_layer
        h1, xt, pq = _outproj(h, yc, ya, w_out_b, g_ffn3, wq_b, i)
        lam, e1z, rho, e2 = _peer_topk(pq, peer_k1, peer_k2, i)
        h2 = _peer_main(xt, u_b, vt_b, lam, e1z, rho, e2, h1, i)
        return h2

    for i in range(depth):
        final = i == depth - 1
        u, bg, q, k, v = _inproj(hp, g_mix3, w_in_b, i, conv_ch, attn_w)
        yc, cst = _conv(u, bg, zero_past, conv_w, i, B, T)
        ya = _attn_prompt(q, k, v, bias_rows, B, T, QB)
        h2 = mixer_tail(hp, yc, ya, i)
        hp = _ple(h2, pp, g_ple3, wg_b, wp_b, g_fin2, i, final)
        kp_l.append(k.reshape(B, T, n_heads, HEAD_DIM)[:, T - keep:])
        vp_l.append(v.reshape(B, T, n_heads, HEAD_DIM)[:, T - keep:])
        cp_l.append(cst)
        u, bg, q, k, v = _inproj(hs, g_mix3, w_in_b, i, conv_ch, attn_w)
        yc, cst = _conv(u, bg, state_conv[i], conv_w, i, S, TS)
        ya = _attn_sample(q, k, v, ck, cv, sbias, smult, i, S, TS)
        h2 = mixer_tail(hs, yc, ya, i)
        hs = _ple(h2, ps, g_ple3, wg_b, wp_b, g_fin2, i, final)
        ks_l.append(k.reshape(S, TS, n_heads, HEAD_DIM))
        vs_l.append(v.reshape(S, TS, n_heads, HEAD_DIM))
        cs_l.append(cst)

    return (hp.reshape(B, T, D), hs.reshape(S, TS, D), jnp.stack(kp_l), jnp.stack(vp_l), jnp.stack(cp_l),
            jnp.stack(ks_l), jnp.stack(vs_l), jnp.stack(cs_l))
```
